```python
import jax, jax.numpy as jnp
from jax import lax
import numpy as np

D_MODEL = 1024
BATCH = 16
SEQ = 2048
DEPTH = 1

MIX_A = D_MODEL // 2
MIX_B = D_MODEL - MIX_A
CHUNK = 128
A_GROUPS = 4
A_GROUP_DIM = MIX_A // A_GROUPS
B_HEAD_DIM = 64
B_HEADS = MIX_B // B_HEAD_DIM
LORA_W = 64
LORA_A = 64
LORA_G = 160
RWKV_COLS = 3 * MIX_B + LORA_W + LORA_A + LORA_G
IN_COLS = 2 * MIX_A + RWKV_COLS
FFN_HIDDEN = ((8 * D_MODEL + 3 * 256 - 1) // (3 * 256)) * 256
ALPHA = (2 * DEPTH) ** 0.25
BETA = (8 * DEPTH) ** -0.25
LN_EPS = 1e-5
GN_EPS = 64e-5

kernel_name = "hybrid_gmlp_rwkv7_deepnorm_adaln"


def layer_norm(x, g, b, eps=LN_EPS):
    xf = x.astype(jnp.float32)
    mu = jnp.mean(xf, axis=-1, keepdims=True)
    var = jnp.mean(jnp.square(xf - mu), axis=-1, keepdims=True)
    y = (xf - mu) * lax.rsqrt(var + eps)
    return (y * g.astype(jnp.float32) + b.astype(jnp.float32)).astype(x.dtype)


def token_shift(p):
    return jnp.pad(p, ((0, 0), (1, 0), (0, 0)))[:, :-1]


def gmlp_spatial_gating(pu, pv, ln_g, ln_b, w_s, b_s):
    B, S, _ = pu.shape
    nc = S // CHUNK
    u = jax.nn.gelu(pu)
    v = layer_norm(jax.nn.gelu(pv), ln_g, ln_b)
    vc = v.reshape(B, nc, CHUNK, A_GROUPS, A_GROUP_DIM)
    mask = jnp.tril(jnp.ones((CHUNK, CHUNK), dtype=w_s.dtype))
    s = jnp.einsum('gts,bnsgd->bntgd', w_s * mask, vc) + b_s.T[None, None, :, :, None]
    return u * s.reshape(B, S, MIX_A)


def _rwkv7_step(S, inp):
    r, w, k, v, a, b = inp
    sa = jnp.einsum('bhij,bhj->bhi', S, a)
    S = S * w[:, :, None, :] + sa[..., None] * b[:, :, None, :] + v[..., None] * k[:, :, None, :]
    y = jnp.einsum('bhij,bhj->bhi', S, r)
    return S, y


def rwkv7_time_mix(p, mu, w0, w_up, a0, a_up, g_up, k_k, k_a, r_k, lnx_g, lnx_b):
    B, S, _ = p.shape
    p = p + (token_shift(p) - p) * mu
    r, k, v = p[..., :MIX_B], p[..., MIX_B:2 * MIX_B], p[..., 2 * MIX_B:3 * MIX_B]
    o = 3 * MIX_B
    wd = p[..., o:o + LORA_W]
    ad = p[..., o + LORA_W:o + LORA_W + LORA_A]
    gd = p[..., o + LORA_W + LORA_A:]
    f32 = jnp.float32
    wlog = -jax.nn.softplus(-(w0 + jnp.tanh(wd) @ w_up).astype(f32)) - 0.5
    decay = jnp.exp(-jnp.exp(wlog))
    a = jax.nn.sigmoid((a0 + ad @ a_up).astype(f32))
    g = jax.nn.sigmoid(gd) @ g_up
    rf, kf, vf = r.astype(f32), k.astype(f32), v.astype(f32)
    heads = lambda t: t.reshape(B, S, B_HEADS, B_HEAD_DIM)
    kk = heads(kf * k_k.astype(f32))
    kk = kk / jnp.maximum(jnp.sqrt(jnp.sum(kk * kk, axis=-1, keepdims=True)), 1e-12)
    kf = kf * (1.0 + (a - 1.0) * k_a.astype(f32))
    rh, kh, vh, wh, ah = heads(rf), heads(kf), heads(vf), heads(decay), heads(a)
    xs = tuple(jnp.swapaxes(t, 0, 1) for t in (rh, wh, kh, vh, -kk, kk * ah))
    S0 = jnp.zeros((B, B_HEADS, B_HEAD_DIM, B_HEAD_DIM), f32)
    _, y = lax.scan(_rwkv7_step, S0, xs)
    y = jnp.swapaxes(y, 0, 1)
    mu_y = jnp.mean(y, axis=-1, keepdims=True)
    var_y = jnp.mean(jnp.square(y - mu_y), axis=-1, keepdims=True)
    y = ((y - mu_y) * lax.rsqrt(var_y + GN_EPS)).reshape(B, S, MIX_B)
    y = y * lnx_g.astype(f32) + lnx_b.astype(f32)
    bonus = jnp.sum(rh * kh * r_k.astype(f32), axis=-1, keepdims=True) * vh
    y = y + bonus.reshape(B, S, MIX_B)
    return (y * g.astype(f32)).astype(p.dtype)


def setup_inputs(seed: int = 0) -> dict:
    key = jax.random.key(seed)
    ks = jax.random.split(key, 32)
    L, D = DEPTH, D_MODEL
    f32 = jnp.float32
    nrm = lambda k, shape, s: jax.random.normal(k, shape, f32) * s
    return {
        "x": nrm(ks[0], (BATCH, SEQ, D), 1.0),
        "c": nrm(ks[1], (BATCH, D), 1.0),
        "emb_ln_g": 1.0 + nrm(ks[2], (D,), 0.05),
        "emb_ln_b": nrm(ks[3], (D,), 0.02),
        "w_ada": nrm(ks[4], (L, D, 6 * D), 0.5 * D ** -0.5),
        "b_ada": nrm(ks[5], (L, 6 * D), 0.02),
        "w_in": nrm(ks[6], (L, D, IN_COLS), D ** -0.5),
        "mu_shift": jax.random.uniform(ks[7], (L, RWKV_COLS), f32),
        "sg_ln_g": 1.0 + nrm(ks[8], (L, MIX_A), 0.05),
        "sg_ln_b": nrm(ks[9], (L, MIX_A), 0.02),
        "w_s": nrm(ks[10], (L, A_GROUPS, CHUNK, CHUNK), CHUNK ** -0.5),
        "b_s": 1.0 + nrm(ks[11], (L, A_GROUPS, CHUNK), 0.1),
        "w0": jax.random.uniform(ks[12], (L, MIX_B), f32, -6.0, 1.0),
        "w_up": nrm(ks[13], (L, LORA_W, MIX_B), LORA_W ** -0.5),
        "a0": nrm(ks[14], (L, MIX_B), 0.1),
        "a_up": nrm(ks[15], (L, LORA_A, MIX_B), LORA_A ** -0.5),
        "g_up": nrm(ks[16], (L, LORA_G, MIX_B), LORA_G ** -0.5),
        "k_k": 0.85 + nrm(ks[17], (L, MIX_B), 0.05),
        "k_a": 1.0 + nrm(ks[18], (L, MIX_B), 0.05),
        "r_k": nrm(ks[19], (L, B_HEADS, B_HEAD_DIM), 0.1),
        "lnx_g": 1.0 + nrm(ks[20], (L, MIX_B), 0.05),
        "lnx_b": nrm(ks[21], (L, MIX_B), 0.02),
        "w_out": nrm(ks[22], (L, D, D), BETA * D ** -0.5),
        "ln1_g": 1.0 + nrm(ks[23], (L, D), 0.05),
        "ln1_b": nrm(ks[24], (L, D), 0.02),
        "w_ffn_in": nrm(ks[25], (L, D, 2 * FFN_HIDDEN), D ** -0.5),
        "w_ffn_out": nrm(ks[26], (L, FFN_HIDDEN, D), BETA * FFN_HIDDEN ** -0.5),
        "ln2_g": 1.0 + nrm(ks[27], (L, D), 0.05),
        "ln2_b": nrm(ks[28], (L, D), 0.02),
    }


def reference(x, c, emb_ln_g, emb_ln_b, w_ada, b_ada, w_in, mu_shift, sg_ln_g, sg_ln_b, w_s, b_s,
              w0, w_up, a0, a_up, g_up, k_k, k_a, r_k, lnx_g, lnx_b, w_out, ln1_g, ln1_b,
              w_ffn_in, w_ffn_out, ln2_g, ln2_b):
    x = layer_norm(x, emb_ln_g, emb_ln_b)
    cs = jax.nn.silu(c)
    for l in range(DEPTH):
        mod = cs @ w_ada[l] + b_ada[l]
        sh1, sc1, gt1, sh2, sc2, gt2 = [m[:, None, :] for m in jnp.split(mod, 6, axis=-1)]
        h = x * (1.0 + sc1) + sh1
        p = h @ w_in[l]
        y_a = gmlp_spatial_gating(p[..., :MIX_A], p[..., MIX_A:2 * MIX_A],
                                  sg_ln_g[l], sg_ln_b[l], w_s[l], b_s[l])
        y_b = rwkv7_time_mix(p[..., 2 * MIX_A:], mu_shift[l], w0[l], w_up[l], a0[l], a_up[l],
                             g_up[l], k_k[l], k_a[l], r_k[l], lnx_g[l], lnx_b[l])
        mix = jnp.concatenate([y_a, y_b], axis=-1) @ w_out[l]
        x = layer_norm(ALPHA * x + gt1 * mix, ln1_g[l], ln1_b[l])
        h = x * (1.0 + sc2) + sh2
        gu = h @ w_ffn_in[l]
        f = (jax.nn.silu(gu[..., :FFN_HIDDEN]) * gu[..., FFN_HIDDEN:]) @ w_ffn_out[l]
        x = layer_norm(ALPHA * x + gt2 * f, ln2_g[l], ln2_b[l])
    return x
```

```python
import functools
import math

import jax
import jax.numpy as jnp
from jax import lax
from jax.experimental import pallas as pl
from jax.experimental.pallas import tpu as pltpu

D_MODEL = 1024
MIX_A = 512
MIX_B = 512
GMLP_CHUNK = 128
A_GROUPS = 4
HEAD_DIM = 64
HEADS = MIX_B // HEAD_DIM
LORA_W = 64
LORA_A = 64
LORA_G = 160
LORA_G_PAD = 256
RWKV_COLS = 3 * MIX_B + LORA_W + LORA_A + LORA_G
RWKV_COLS_PAD = 3 * MIX_B + LORA_W + LORA_A + LORA_G_PAD
IN_COLS_PAD = 2 * MIX_A + RWKV_COLS_PAD
FFN_HIDDEN = 2816
FFN_CHUNK = 256
DEPTH = 1
ALPHA = (2 * DEPTH) ** 0.25
LN_EPS = 1e-5
GN_EPS = 64e-5
DECAY_SCALE = math.exp(-0.5)

RCHUNK = 64
SEQ_TILE = 256
FFN_TILE = 512
VMEM_LIMIT_BYTES = 56 * 1024 * 1024

_F32 = jnp.float32
_BF16 = jnp.bfloat16


def _layer_norm(x, g, b, eps):
    mu = jnp.mean(x, axis=-1, keepdims=True)
    d = x - mu
    var = jnp.mean(d * d, axis=-1, keepdims=True)
    return d * lax.rsqrt(var + eps) * g + b


def _gelu_tanh(x):
    c = math.sqrt(2.0 / math.pi)
    return 0.5 * x * (1.0 + jnp.tanh(c * (x + 0.044715 * (x * x * x))))


def _dot(a, b):
    return jnp.dot(a.astype(_BF16), b.astype(_BF16), preferred_element_type=_F32)


def _dot_nt(a, b):
    return lax.dot_general(a.astype(_BF16), b.astype(_BF16), (((1,), (1,)), ((), ())),
                           preferred_element_type=_F32)


def _dot_tn(a, b):
    return lax.dot_general(a.astype(_BF16), b.astype(_BF16), (((0,), (0,)), ((), ())),
                           preferred_element_type=_F32)


def _ada_kernel(c_ref, w_ref, b_ref, o_ref):
    c = c_ref[...]
    cs = c * jax.nn.sigmoid(c)
    o_ref[...] = jnp.dot(cs, w_ref[...], preferred_element_type=_F32,
                         precision=lax.Precision.HIGHEST) + b_ref[...]


def _ada_call(c, w_ada, b_ada):
    B, D = c.shape
    n = w_ada.shape[1]
    tn = 1024
    return pl.pallas_call(
        _ada_kernel,
        grid=(n // tn,),
        in_specs=[pl.BlockSpec((B, D), lambda j: (0, 0)),
                  pl.BlockSpec((D, tn), lambda j: (0, j)),
                  pl.BlockSpec((1, tn), lambda j: (0, j))],
        out_specs=pl.BlockSpec((B, tn), lambda j: (0, j)),
        out_shape=jax.ShapeDtypeStruct((B, n), _F32),
        name="ada_mod",
    )(c, w_ada, b_ada.reshape(1, n))


def _mix_kernel(x_ref, mod_ref, eg_ref, eb_ref, win_ref, mu_ref, sgg_ref, sgb_ref, ws_ref, bs_ref,
                w0_ref, wwa_ref, a0_ref, gup_ref, kk_ref, ka_ref, rk_ref, lg_ref, lb_ref,
                y_ref,
                carry_ref, state_ref, at_ref, rt_ref, bt_ref, kt_ref, v_ref, cum_ref, yb_ref):
    ts = x_ref.shape[1]
    t_idx = pl.program_id(1)

    @pl.when(t_idx == 0)
    def _():
        carry_ref[...] = jnp.zeros_like(carry_ref)
        state_ref[...] = jnp.zeros_like(state_ref)

    x0 = _layer_norm(x_ref[0], eg_ref[...], eb_ref[...], LN_EPS)
    sh1 = mod_ref[0, 0:1, :]
    sc1 = mod_ref[0, 1:2, :]
    h = x0 * (1.0 + sc1) + sh1
    p = _dot(h, win_ref[...])

    u = _gelu_tanh(p[:, 0:MIX_A])
    vg = _layer_norm(_gelu_tanh(p[:, MIX_A:2 * MIX_A]), sgg_ref[...], sgb_ref[...], LN_EPS)
    row = lax.broadcasted_iota(jnp.int32, (GMLP_CHUNK, GMLP_CHUNK), 0)
    col = lax.broadcasted_iota(jnp.int32, (GMLP_CHUNK, GMLP_CHUNK), 1)
    tril = row >= col
    for g in range(A_GROUPS):
        wm = jnp.where(tril, ws_ref[g], 0.0)
        for c in range(ts // GMLP_CHUNK):
            rs = slice(c * GMLP_CHUNK, (c + 1) * GMLP_CHUNK)
            cs = slice(g * GMLP_CHUNK, (g + 1) * GMLP_CHUNK)
            s = _dot(wm, vg[rs, cs]) + bs_ref[g]
            y_ref[0, rs, cs] = (u[rs, cs] * s).astype(y_ref.dtype)

    ps = p[:, 2 * MIX_A:]
    rolled = pltpu.roll(ps, 1, 0)
    rid = lax.broadcasted_iota(jnp.int32, ps.shape, 0)
    prev = jnp.where(rid == 0, carry_ref[0:1, :], rolled)
    carry_ref[0:1, :] = ps[ts - 1:ts, :]
    xs = ps + (prev - ps) * mu_ref[...]

    r = xs[:, 0:MIX_B]
    k = xs[:, MIX_B:2 * MIX_B]
    v = xs[:, 2 * MIX_B:3 * MIX_B]
    wa = xs[:, 3 * MIX_B:3 * MIX_B + 128]
    gd = xs[:, 3 * MIX_B + 128:]

    lane = lax.broadcasted_iota(jnp.int32, wa.shape, 1)
    wa_act = jnp.where(lane < LORA_W, jnp.tanh(wa), wa)
    up = _dot(wa_act, wwa_ref[...])
    sig_w = jax.nn.sigmoid(w0_ref[...] + up[:, 0:MIX_B])
    lr = jax.nn.sigmoid(a0_ref[...] + up[:, MIX_B:])
    gate = _dot(jax.nn.sigmoid(gd), gup_ref[...])

    hr = jnp.right_shift(lax.broadcasted_iota(jnp.int32, (MIX_B, MIX_B), 0), 6)
    hc = jnp.right_shift(lax.broadcasted_iota(jnp.int32, (MIX_B, MIX_B), 1), 6)
    head_ones = jnp.where(hr == hc, 1.0, 0.0).astype(_BF16)

    kk = k * kk_ref[...]
    kk = kk * lax.rsqrt(jnp.maximum(_dot(kk * kk, head_ones), 1e-24))
    k2 = k * (1.0 + (lr - 1.0) * ka_ref[...])
    bonus = _dot(r * k2 * rk_ref[...], head_ones) * v

    ri = lax.broadcasted_iota(jnp.int32, (ts, ts), 0)
    ci = lax.broadcasted_iota(jnp.int32, (ts, ts), 1)
    lmat = jnp.where((jnp.right_shift(ri, 6) == jnp.right_shift(ci, 6)) & (ri >= ci),
                     1.0, 0.0).astype(_BF16)
    sw_hi = sig_w.astype(_BF16)
    sw_lo = (sig_w - sw_hi.astype(_F32)).astype(_BF16)
    csum = (jnp.dot(lmat, sw_hi, preferred_element_type=_F32)
            + jnp.dot(lmat, sw_lo, preferred_element_type=_F32))
    cum = -DECAY_SCALE * csum
    e_pos = jnp.exp(cum)
    e_neg = jnp.exp(-cum)
    e_exc = jnp.exp(cum + DECAY_SCALE * sig_w)
    at_ref[...] = -kk * e_exc
    rt_ref[...] = r * e_pos
    bt_ref[...] = kk * lr * e_neg
    kt_ref[...] = k2 * e_neg
    v_ref[...] = v
    cum_ref[...] = cum

    gi = lax.broadcasted_iota(jnp.int32, (2 * RCHUNK, 2 * RCHUNK), 0)
    gj = lax.broadcasted_iota(jnp.int32, (2 * RCHUNK, 2 * RCHUNK), 1)
    gim = jnp.bitwise_and(gi, RCHUNK - 1)
    gjm = jnp.bitwise_and(gj, RCHUNK - 1)
    gmask = (gim > gjm) | ((gi >= RCHUNK) & (gim == gjm))
    zlane = lax.broadcasted_iota(jnp.int32, (RCHUNK, 2 * RCHUNK), 1)

    def chunk_body(c, carry):
        r0 = pl.multiple_of(c * RCHUNK, RCHUNK)
        rows = pl.ds(r0, RCHUNK)
        cum_c = cum_ref[rows, :]
        w_end = jnp.exp(cum_c[RCHUNK - 1:RCHUNK, :])
        at_c = at_ref[rows, :]
        rt_c = rt_ref[rows, :]
        bt_c = bt_ref[rows, :]
        kt_c = kt_ref[rows, :]
        v_c = v_ref[rows, :]
        s_all = state_ref[...]
        hs = range(HEADS)
        sl = [slice(hd * HEAD_DIM, (hd + 1) * HEAD_DIM) for hd in hs]
        lhs = [jnp.concatenate([at_c[:, s], rt_c[:, s]], axis=0).astype(_BF16) for s in sl]
        rhs = [jnp.concatenate([bt_c[:, s], kt_c[:, s]], axis=0) for s in sl]
        s0 = [s_all[:, s] for s in sl]
        vh = [v_c[:, s] for s in sl]
        g = [_dot_nt(lhs[i], rhs[i]) for i in hs]
        ah = [_dot_nt(lhs[i], s0[i]) for i in hs]
        gm = [jnp.where(gmask, g[i], 0.0).astype(_BF16) for i in hs]
        akv = [_dot(gm[i][0:RCHUNK, RCHUNK:], vh[i]) for i in hs]
        z = [jnp.concatenate([gm[i][0:RCHUNK, 0:RCHUNK].astype(_F32), ah[i][0:RCHUNK] + akv[i]],
                             axis=1) for i in hs]
        for lvl in range(6):
            prod = [_dot(z[i][:, 0:RCHUNK], z[i]) for i in hs]
            z = [prod[i] + jnp.where(zlane >= RCHUNK, z[i], 0.0) for i in hs]
        uv = [jnp.concatenate([z[i][:, RCHUNK:], vh[i]], axis=0).astype(_BF16) for i in hs]
        y_cols = [ah[i][RCHUNK:] + _dot(gm[i][RCHUNK:], uv[i]) for i in hs]
        bk = [rhs[i] * w_end[:, sl[i]] for i in hs]
        new_cols = [s0[i] * w_end[:, sl[i]] + _dot_tn(uv[i], bk[i]) for i in hs]
        state_ref[...] = jnp.concatenate(new_cols, axis=1)
        yb_ref[rows, :] = jnp.concatenate(y_cols, axis=1)
        return carry

    lax.fori_loop(0, ts // RCHUNK, chunk_body, 0)

    yv = yb_ref[...]
    inv = 1.0 / HEAD_DIM
    mean = _dot(yv, head_ones) * inv
    d = yv - mean
    var = _dot(d * d, head_ones) * inv
    yn = d * lax.rsqrt(var + GN_EPS) * lg_ref[...] + lb_ref[...]
    y_ref[0, :, MIX_A:] = ((yn + bonus) * gate).astype(y_ref.dtype)


def _mix_call(x, mod3, eg, eb, win_p, mu_p, sgg, sgb, ws, bs_b, w0, wwa, a0, gup_p, kkp, kap, rkp,
              lg, lb):
    B, S, D = x.shape
    ts = SEQ_TILE
    const2 = lambda b, t: (0, 0)
    const3 = lambda b, t: (0, 0, 0)

    def full(a):
        return pl.BlockSpec(a.shape, const2 if a.ndim == 2 else const3)

    in_specs = [pl.BlockSpec((1, ts, D), lambda b, t: (b, t, 0)),
                pl.BlockSpec((1, 6, D), lambda b, t: (b, 0, 0))]
    consts = [eg, eb, win_p, mu_p, sgg, sgb, ws, bs_b, w0, wwa, a0, gup_p, kkp, kap, rkp, lg, lb]
    in_specs += [full(a) for a in consts]
    scratch = [pltpu.VMEM((8, RWKV_COLS_PAD), _F32),
               pltpu.VMEM((HEAD_DIM, MIX_B), _F32),
               pltpu.VMEM((ts, MIX_B), _F32),
               pltpu.VMEM((ts, MIX_B), _F32),
               pltpu.VMEM((ts, MIX_B), _F32),
               pltpu.VMEM((ts, MIX_B), _F32),
               pltpu.VMEM((ts, MIX_B), _F32),
               pltpu.VMEM((ts, MIX_B), _F32),
               pltpu.VMEM((ts, MIX_B), _F32)]
    return pl.pallas_call(
        _mix_kernel,
        grid=(B, S // ts),
        in_specs=in_specs,
        out_specs=pl.BlockSpec((1, ts, D), lambda b, t: (b, t, 0)),
        out_shape=jax.ShapeDtypeStruct((B, S, D), _BF16),
        scratch_shapes=scratch,
        compiler_params=pltpu.CompilerParams(
            dimension_semantics=("arbitrary", "arbitrary"),
            vmem_limit_bytes=VMEM_LIMIT_BYTES),
        name="token_mix",
    )(x, mod3, *consts)


def _ffn_kernel(x_ref, y_ref, mod_ref, eg_ref, eb_ref, wout_ref, g1_ref, b1_ref, wfi_ref, wfo_ref,
                g2_ref, b2_ref, o_ref, act_ref):
    x0 = _layer_norm(x_ref[0], eg_ref[...], eb_ref[...], LN_EPS)
    gt1 = mod_ref[0, 2:3, :]
    sh2 = mod_ref[0, 3:4, :]
    sc2 = mod_ref[0, 4:5, :]
    gt2 = mod_ref[0, 5:6, :]
    mix = jnp.dot(y_ref[0], wout_ref[...], preferred_element_type=_F32)
    x1 = _layer_norm(ALPHA * x0 + gt1 * mix, g1_ref[...], b1_ref[...], LN_EPS)
    h = (x1 * (1.0 + sc2) + sh2).astype(_BF16)
    for c in range(FFN_HIDDEN // FFN_CHUNK):
        gcol = pl.ds(c * FFN_CHUNK, FFN_CHUNK)
        ucol = pl.ds(FFN_HIDDEN + c * FFN_CHUNK, FFN_CHUNK)
        gg = jnp.dot(h, wfi_ref[:, gcol], preferred_element_type=_F32)
        uu = jnp.dot(h, wfi_ref[:, ucol], preferred_element_type=_F32)
        act_ref[:, gcol] = (gg * jax.nn.sigmoid(gg) * uu).astype(_BF16)
    f = jnp.dot(act_ref[...], wfo_ref[...], preferred_element_type=_F32)
    o_ref[0] = _layer_norm(ALPHA * x1 + gt2 * f, g2_ref[...], b2_ref[...], LN_EPS)


def _ffn_call(x, y, mod3, eg, eb, wout, g1, b1, wfi, wfo, g2, b2):
    B, S, D = x.shape
    tm = FFN_TILE
    const2 = lambda b, t: (0, 0)
    tile = lambda b, t: (b, t, 0)

    def resident(a):
        return pl.BlockSpec(a.shape, const2, pipeline_mode=pl.Buffered(1))

    return pl.pallas_call(
        _ffn_kernel,
        grid=(B, S // tm),
        in_specs=[pl.BlockSpec((1, tm, D), tile),
                  pl.BlockSpec((1, tm, D), tile),
                  pl.BlockSpec((1, 6, D), lambda b, t: (b, 0, 0)),
                  resident(eg), resident(eb), resident(wout), resident(g1), resident(b1),
                  resident(wfi), resident(wfo), resident(g2), resident(b2)],
        out_specs=pl.BlockSpec((1, tm, D), tile),
        out_shape=jax.ShapeDtypeStruct((B, S, D), _F32),
        scratch_shapes=[pltpu.VMEM((tm, FFN_HIDDEN), _BF16)],
        compiler_params=pltpu.CompilerParams(
            dimension_semantics=("arbitrary", "arbitrary"),
            vmem_limit_bytes=VMEM_LIMIT_BYTES),
        name="proj_ffn",
    )(x, y, mod3, eg, eb, wout, g1, b1, wfi, wfo, g2, b2)


def kernel(x, c, emb_ln_g, emb_ln_b, w_ada, b_ada, w_in, mu_shift, sg_ln_g, sg_ln_b, w_s, b_s,
           w0, w_up, a0, a_up, g_up, k_k, k_a, r_k, lnx_g, lnx_b, w_out, ln1_g, ln1_b,
           w_ffn_in, w_ffn_out, ln2_g, ln2_b):
    B, S, D = x.shape
    row = lambda a: a.reshape(1, -1).astype(_F32)
    l = 0
    pad_cols = RWKV_COLS_PAD - RWKV_COLS

    win_p = jnp.pad(w_in[l], ((0, 0), (0, pad_cols))).astype(_BF16)
    mu_p = jnp.pad(mu_shift[l], (0, pad_cols)).reshape(1, -1)
    wwa = jnp.zeros((LORA_W + LORA_A, 2 * MIX_B), _F32)
    wwa = wwa.at[:LORA_W, :MIX_B].set(w_up[l]).at[LORA_W:, MIX_B:].set(a_up[l]).astype(_BF16)
    gup_p = jnp.pad(g_up[l], ((0, LORA_G_PAD - LORA_G), (0, 0))).astype(_BF16)
    bs_b = jnp.broadcast_to(b_s[l][:, :, None], (A_GROUPS, GMLP_CHUNK, GMLP_CHUNK))

    mod3 = _ada_call(c, w_ada[l], b_ada[l]).reshape(B, 6, D)
    y = _mix_call(x, mod3, row(emb_ln_g), row(emb_ln_b), win_p, mu_p, row(sg_ln_g[l]),
                  row(sg_ln_b[l]), w_s[l], bs_b, row(w0[l]), wwa, row(a0[l]), gup_p,
                  row(k_k[l]), row(k_a[l]), row(r_k[l]), row(lnx_g[l]), row(lnx_b[l]))
    return _ffn_call(x, y, mod3, row(emb_ln_g), row(emb_ln_b), w_out[l].astype(_BF16),
                     row(ln1_g[l]), row(ln1_b[l]), w_ffn_in[l].astype(_BF16),
                     w_ffn_out[l].astype(_BF16), row(ln2_g[l]), row(ln2_b[l]))
```

```python
import math

import jax
import jax.numpy as jnp
from jax import lax
from jax.experimental import pallas as pl
from jax.experimental.pallas import tpu as pltpu

D_MODEL = 1024
MIX_A = 512
MIX_B = 512
GMLP_CHUNK = 128
A_GROUPS = 4
HEAD_DIM = 64
HEADS = MIX_B // HEAD_DIM
LORA_W = 64
LORA_A = 64
LORA_G = 160
LORA_G_PAD = 256
RWKV_COLS = 3 * MIX_B + LORA_W + LORA_A + LORA_G
RWKV_COLS_PAD = 3 * MIX_B + LORA_W + LORA_A + LORA_G_PAD
IN_COLS_PAD = 2 * MIX_A + RWKV_COLS_PAD
FFN_HIDDEN = 2816
FFN_CHUNK = 256
DEPTH = 1
ALPHA = (2 * DEPTH) ** 0.25
LN_EPS = 1e-5
GN_EPS = 64e-5
DECAY_SCALE = math.exp(-0.5)

RCHUNK = 64
CUM_BLOCK = 256
MIX_BATCH = 4
SEQ_TILE = 128
FFN_TILE = 512
VMEM_LIMIT_BYTES = 56 * 1024 * 1024

_F32 = jnp.float32
_BF16 = jnp.bfloat16


def _layer_norm(x, g, b, eps):
    mu = jnp.mean(x, axis=-1, keepdims=True)
    d = x - mu
    var = jnp.mean(d * d, axis=-1, keepdims=True)
    return d * lax.rsqrt(var + eps) * g + b


def _gelu_tanh(x):
    c = math.sqrt(2.0 / math.pi)
    return 0.5 * x * (1.0 + jnp.tanh(c * (x + 0.044715 * (x * x * x))))


def _dot(a, b):
    return jnp.dot(a.astype(_BF16), b.astype(_BF16), preferred_element_type=_F32)


def _dot_nt(a, b):
    return lax.dot_general(a.astype(_BF16), b.astype(_BF16), (((1,), (1,)), ((), ())),
                           preferred_element_type=_F32)


def _dot_tn(a, b):
    return lax.dot_general(a.astype(_BF16), b.astype(_BF16), (((0,), (0,)), ((), ())),
                           preferred_element_type=_F32)


def _ada_kernel(c_ref, w_ref, b_ref, o_ref):
    c = c_ref[...]
    cs = c * jax.nn.sigmoid(c)
    o_ref[...] = jnp.dot(cs, w_ref[...], preferred_element_type=_F32,
                         precision=lax.Precision.HIGHEST) + b_ref[...]


def _ada_call(c, w_ada, b_ada):
    B, D = c.shape
    n = w_ada.shape[1]
    tn = 1024
    return pl.pallas_call(
        _ada_kernel,
        grid=(n // tn,),
        in_specs=[pl.BlockSpec((B, D), lambda j: (0, 0)),
                  pl.BlockSpec((D, tn), lambda j: (0, j)),
                  pl.BlockSpec((1, tn), lambda j: (0, j))],
        out_specs=pl.BlockSpec((B, tn), lambda j: (0, j)),
        out_shape=jax.ShapeDtypeStruct((B, n), _F32),
        name="ada_mod",
    )(c, w_ada, b_ada.reshape(1, n))


def _mix_kernel(x_ref, mod_ref, eg_ref, eb_ref, win_ref, mu_ref, sgg_ref, sgb_ref, ws_ref, bs_ref,
                w0_ref, wwa_ref, a0_ref, gup_ref, kk_ref, ka_ref, rk_ref, lg_ref, lb_ref,
                y_ref,
                carry_ref, state_ref, at_ref, rt_ref, bt_ref, kt_ref, v_ref, cum_ref, yb_ref):
    nb, ts = x_ref.shape[0], x_ref.shape[1]
    rows_all = nb * ts
    t_idx = pl.program_id(1)

    @pl.when(t_idx == 0)
    def _():
        carry_ref[...] = jnp.zeros_like(carry_ref)
        state_ref[...] = jnp.zeros_like(state_ref)

    h_parts = []
    for b in range(nb):
        x0 = _layer_norm(x_ref[b], eg_ref[...], eb_ref[...], LN_EPS)
        h_parts.append((x0 * (1.0 + mod_ref[b, 1:2, :]) + mod_ref[b, 0:1, :]).astype(_BF16))
    h = jnp.concatenate(h_parts, axis=0)
    p = jnp.dot(h, win_ref[...], preferred_element_type=_F32)

    u = _gelu_tanh(p[:, 0:MIX_A])
    vg = _layer_norm(_gelu_tanh(p[:, MIX_A:2 * MIX_A]), sgg_ref[...], sgb_ref[...], LN_EPS)
    row = lax.broadcasted_iota(jnp.int32, (GMLP_CHUNK, GMLP_CHUNK), 0)
    col = lax.broadcasted_iota(jnp.int32, (GMLP_CHUNK, GMLP_CHUNK), 1)
    tril = row >= col
    for g in range(A_GROUPS):
        wm = jnp.where(tril, ws_ref[g], 0.0).astype(_BF16)
        cs = slice(g * GMLP_CHUNK, (g + 1) * GMLP_CHUNK)
        for c in range(rows_all // GMLP_CHUNK):
            b, r0 = divmod(c * GMLP_CHUNK, ts)
            rs = slice(c * GMLP_CHUNK, (c + 1) * GMLP_CHUNK)
            s = _dot(wm, vg[rs, cs]) + bs_ref[g]
            y_ref[b, r0:r0 + GMLP_CHUNK, cs] = (u[rs, cs] * s).astype(y_ref.dtype)

    ps = p[:, 2 * MIX_A:]
    prev = pltpu.roll(ps, 1, 0)
    rid = lax.broadcasted_iota(jnp.int32, ps.shape, 0)
    for b in range(nb):
        prev = jnp.where(rid == b * ts, carry_ref[8 * b:8 * b + 1, :], prev)
    for b in range(nb):
        carry_ref[8 * b:8 * b + 1, :] = ps[(b + 1) * ts - 1:(b + 1) * ts, :]
    xs = ps + (prev - ps) * mu_ref[...]

    r = xs[:, 0:MIX_B]
    k = xs[:, MIX_B:2 * MIX_B]
    v = xs[:, 2 * MIX_B:3 * MIX_B]
    wa = xs[:, 3 * MIX_B:3 * MIX_B + 128]
    gd = xs[:, 3 * MIX_B + 128:]

    lane = lax.broadcasted_iota(jnp.int32, wa.shape, 1)
    wa_act = jnp.where(lane < LORA_W, jnp.tanh(wa), wa)
    up = _dot(wa_act, wwa_ref[...])
    sig_w = jax.nn.sigmoid(w0_ref[...] + up[:, 0:MIX_B])
    lr = jax.nn.sigmoid(a0_ref[...] + up[:, MIX_B:])
    gate = _dot(jax.nn.sigmoid(gd), gup_ref[...])

    hr = jnp.right_shift(lax.broadcasted_iota(jnp.int32, (MIX_B, MIX_B), 0), 6)
    hc = jnp.right_shift(lax.broadcasted_iota(jnp.int32, (MIX_B, MIX_B), 1), 6)
    head_ones = jnp.where(hr == hc, 1.0, 0.0).astype(_BF16)

    kk = k * kk_ref[...]
    kk = kk * lax.rsqrt(jnp.maximum(_dot(kk * kk, head_ones), 1e-24))
    k2 = k * (1.0 + (lr - 1.0) * ka_ref[...])
    bonus = _dot(r * k2 * rk_ref[...], head_ones) * v

    cb = min(CUM_BLOCK, rows_all)
    ri = lax.broadcasted_iota(jnp.int32, (cb, cb), 0)
    ci = lax.broadcasted_iota(jnp.int32, (cb, cb), 1)
    lmat = jnp.where((jnp.right_shift(ri, 6) == jnp.right_shift(ci, 6)) & (ri >= ci),
                     1.0, 0.0).astype(_BF16)
    sw_hi = sig_w.astype(_BF16)
    sw_lo = (sig_w - sw_hi.astype(_F32)).astype(_BF16)
    csum = jnp.concatenate(
        [jnp.dot(lmat, sw_hi[i:i + cb], preferred_element_type=_F32)
         + jnp.dot(lmat, sw_lo[i:i + cb], preferred_element_type=_F32)
         for i in range(0, rows_all, cb)], axis=0)
    cum = -DECAY_SCALE * csum
    e_pos = jnp.exp(cum)
    e_neg = jnp.exp(-cum)
    e_exc = jnp.exp(cum + DECAY_SCALE * sig_w)
    at_ref[...] = -kk * e_exc
    rt_ref[...] = r * e_pos
    bt_ref[...] = kk * lr * e_neg
    kt_ref[...] = k2 * e_neg
    v_ref[...] = v
    cum_ref[...] = cum

    gi = lax.broadcasted_iota(jnp.int32, (2 * RCHUNK, 2 * RCHUNK), 0)
    gj = lax.broadcasted_iota(jnp.int32, (2 * RCHUNK, 2 * RCHUNK), 1)
    gim = jnp.bitwise_and(gi, RCHUNK - 1)
    gjm = jnp.bitwise_and(gj, RCHUNK - 1)
    gmask = (gim > gjm) | ((gi >= RCHUNK) & (gim == gjm))
    zlane = lax.broadcasted_iota(jnp.int32, (RCHUNK, 2 * RCHUNK), 1)
    lanes = [slice(hd * HEAD_DIM, (hd + 1) * HEAD_DIM) for hd in range(HEADS)]

    def chunk_body(c, carry):
        lhs, rhs, s0, vh, wend = [], [], [], [], []
        for b in range(nb):
            rows = pl.ds(pl.multiple_of(b * ts + c * RCHUNK, RCHUNK), RCHUNK)
            at_c = at_ref[rows, :]
            rt_c = rt_ref[rows, :]
            bt_c = bt_ref[rows, :]
            kt_c = kt_ref[rows, :]
            v_c = v_ref[rows, :]
            s_all = state_ref[b * HEAD_DIM:(b + 1) * HEAD_DIM, :]
            w_end = jnp.exp(cum_ref[pl.ds(b * ts + c * RCHUNK + RCHUNK - 1, 1), :])
            for s in lanes:
                lhs.append(jnp.concatenate([at_c[:, s], rt_c[:, s]], axis=0).astype(_BF16))
                rhs.append(jnp.concatenate([bt_c[:, s], kt_c[:, s]], axis=0))
                s0.append(s_all[:, s])
                vh.append(v_c[:, s])
                wend.append(w_end[:, s])
        n = range(len(lhs))
        g = [_dot_nt(lhs[i], rhs[i]) for i in n]
        ah = [_dot_nt(lhs[i], s0[i]) for i in n]
        gm = [jnp.where(gmask, g[i], 0.0).astype(_BF16) for i in n]
        akv = [_dot(gm[i][0:RCHUNK, RCHUNK:], vh[i]) for i in n]
        z = [jnp.concatenate([gm[i][0:RCHUNK, 0:RCHUNK].astype(_F32), ah[i][0:RCHUNK] + akv[i]],
                             axis=1) for i in n]
        for _ in range(6):
            prod = [_dot(z[i][:, 0:RCHUNK], z[i]) for i in n]
            z = [prod[i] + jnp.where(zlane >= RCHUNK, z[i], 0.0) for i in n]
        uv = [jnp.concatenate([z[i][:, RCHUNK:], vh[i]], axis=0).astype(_BF16) for i in n]
        y_cols = [ah[i][RCHUNK:] + _dot(gm[i][RCHUNK:], uv[i]) for i in n]
        new_cols = [s0[i] * wend[i] + _dot_tn(uv[i], rhs[i] * wend[i]) for i in n]
        for b in range(nb):
            sl = slice(b * HEADS, (b + 1) * HEADS)
            rows = pl.ds(pl.multiple_of(b * ts + c * RCHUNK, RCHUNK), RCHUNK)
            state_ref[b * HEAD_DIM:(b + 1) * HEAD_DIM, :] = jnp.concatenate(new_cols[sl], axis=1)
            yb_ref[rows, :] = jnp.concatenate(y_cols[sl], axis=1)
        return carry

    lax.fori_loop(0, ts // RCHUNK, chunk_body, 0)

    yv = yb_ref[...]
    inv = 1.0 / HEAD_DIM
    mean = _dot(yv, head_ones) * inv
    d = yv - mean
    var = _dot(d * d, head_ones) * inv
    yn = d * lax.rsqrt(var + GN_EPS) * lg_ref[...] + lb_ref[...]
    yb = ((yn + bonus) * gate).astype(y_ref.dtype)
    for b in range(nb):
        y_ref[b, :, MIX_A:] = yb[b * ts:(b + 1) * ts]


def _mix_call(x, mod3, eg, eb, win_p, mu_p, sgg, sgb, ws, bs_b, w0, wwa, a0, gup_p, kkp, kap, rkp,
              lg, lb):
    B, S, D = x.shape
    nb = math.gcd(B, MIX_BATCH)
    ts = SEQ_TILE
    rows = nb * ts
    const2 = lambda b, t: (0, 0)
    const3 = lambda b, t: (0, 0, 0)

    def full(a):
        return pl.BlockSpec(a.shape, const2 if a.ndim == 2 else const3)

    in_specs = [pl.BlockSpec((nb, ts, D), lambda b, t: (b, t, 0)),
                pl.BlockSpec((nb, 6, D), lambda b, t: (b, 0, 0))]
    consts = [eg, eb, win_p, mu_p, sgg, sgb, ws, bs_b, w0, wwa, a0, gup_p, kkp, kap, rkp, lg, lb]
    in_specs += [full(a) for a in consts]
    scratch = [pltpu.VMEM((8 * nb, RWKV_COLS_PAD), _F32),
               pltpu.VMEM((nb * HEAD_DIM, MIX_B), _F32),
               pltpu.VMEM((rows, MIX_B), _F32),
               pltpu.VMEM((rows, MIX_B), _F32),
               pltpu.VMEM((rows, MIX_B), _F32),
               pltpu.VMEM((rows, MIX_B), _F32),
               pltpu.VMEM((rows, MIX_B), _F32),
               pltpu.VMEM((rows, MIX_B), _F32),
               pltpu.VMEM((rows, MIX_B), _F32)]
    return pl.pallas_call(
        _mix_kernel,
        grid=(B // nb, S // ts),
        in_specs=in_specs,
        out_specs=pl.BlockSpec((nb, ts, D), lambda b, t: (b, t, 0)),
        out_shape=jax.ShapeDtypeStruct((B, S, D), _BF16),
        scratch_shapes=scratch,
        compiler_params=pltpu.CompilerParams(
            dimension_semantics=("arbitrary", "arbitrary"),
            vmem_limit_bytes=VMEM_LIMIT_BYTES),
        name="token_mix",
    )(x, mod3, *consts)


def _ffn_kernel(x_ref, y_ref, mod_ref, eg_ref, eb_ref, wout_ref, g1_ref, b1_ref, wfi_ref, wfo_ref,
                g2_ref, b2_ref, o_ref, act_ref):
    x0 = _layer_norm(x_ref[0], eg_ref[...], eb_ref[...], LN_EPS)
    gt1 = mod_ref[0, 2:3, :]
    sh2 = mod_ref[0, 3:4, :]
    sc2 = mod_ref[0, 4:5, :]
    gt2 = mod_ref[0, 5:6, :]
    mix = jnp.dot(y_ref[0], wout_ref[...], preferred_element_type=_F32)
    x1 = _layer_norm(ALPHA * x0 + gt1 * mix, g1_ref[...], b1_ref[...], LN_EPS)
    h = (x1 * (1.0 + sc2) + sh2).astype(_BF16)
    for c in range(FFN_HIDDEN // FFN_CHUNK):
        gcol = pl.ds(c * FFN_CHUNK, FFN_CHUNK)
        ucol = pl.ds(FFN_HIDDEN + c * FFN_CHUNK, FFN_CHUNK)
        gg = jnp.dot(h, wfi_ref[:, gcol], preferred_element_type=_F32)
        uu = jnp.dot(h, wfi_ref[:, ucol], preferred_element_type=_F32)
        act_ref[:, gcol] = (gg * jax.nn.sigmoid(gg) * uu).astype(_BF16)
    f = jnp.dot(act_ref[...], wfo_ref[...], preferred_element_type=_F32)
    o_ref[0] = _layer_norm(ALPHA * x1 + gt2 * f, g2_ref[...], b2_ref[...], LN_EPS)


def _ffn_call(x, y, mod3, eg, eb, wout, g1, b1, wfi, wfo, g2, b2):
    B, S, D = x.shape
    tm = FFN_TILE
    const2 = lambda b, t: (0, 0)
    tile = lambda b, t: (b, t, 0)

    def resident(a):
        return pl.BlockSpec(a.shape, const2, pipeline_mode=pl.Buffered(1))

    return pl.pallas_call(
        _ffn_kernel,
        grid=(B, S // tm),
        in_specs=[pl.BlockSpec((1, tm, D), tile),
                  pl.BlockSpec((1, tm, D), tile),
                  pl.BlockSpec((1, 6, D), lambda b, t: (b, 0, 0)),
                  resident(eg), resident(eb), resident(wout), resident(g1), resident(b1),
                  resident(wfi), resident(wfo), resident(g2), resident(b2)],
        out_specs=pl.BlockSpec((1, tm, D), tile),
        out_shape=jax.ShapeDtypeStruct((B, S, D), _F32),
        scratch_shapes=[pltpu.VMEM((tm, FFN_HIDDEN), _BF16)],
        compiler_params=pltpu.CompilerParams(
            dimension_semantics=("arbitrary", "arbitrary"),
            vmem_limit_bytes=VMEM_LIMIT_BYTES),
        name="proj_ffn",
    )(x, y, mod3, eg, eb, wout, g1, b1, wfi, wfo, g2, b2)


def kernel(x, c, emb_ln_g, emb_ln_b, w_ada, b_ada, w_in, mu_shift, sg_ln_g, sg_ln_b, w_s, b_s,
           w0, w_up, a0, a_up, g_up, k_k, k_a, r_k, lnx_g, lnx_b, w_out, ln1_g, ln1_b,
           w_ffn_in, w_ffn_out, ln2_g, ln2_b):
    B, S, D = x.shape
    row = lambda a: a.reshape(1, -1).astype(_F32)
    l = 0
    pad_cols = RWKV_COLS_PAD - RWKV_COLS

    win_p = jnp.pad(w_in[l], ((0, 0), (0, pad_cols))).astype(_BF16)
    mu_p = jnp.pad(mu_shift[l], (0, pad_cols)).reshape(1, -1)
    wwa = jnp.zeros((LORA_W + LORA_A, 2 * MIX_B), _F32)
    wwa = wwa.at[:LORA_W, :MIX_B].set(w_up[l]).at[LORA_W:, MIX_B:].set(a_up[l]).astype(_BF16)
    gup_p = jnp.pad(g_up[l], ((0, LORA_G_PAD - LORA_G), (0, 0))).astype(_BF16)
    bs_b = jnp.broadcast_to(b_s[l][:, :, None], (A_GROUPS, GMLP_CHUNK, GMLP_CHUNK))

    mod3 = _ada_call(c, w_ada[l], b_ada[l]).reshape(B, 6, D)
    y = _mix_call(x, mod3, row(emb_ln_g), row(emb_ln_b), win_p, mu_p, row(sg_ln_g[l]),
                  row(sg_ln_b[l]), w_s[l], bs_b, row(w0[l]), wwa, row(a0[l]), gup_p,
                  row(k_k[l]), row(k_a[l]), row(r_k[l]), row(lnx_g[l]), row(lnx_b[l]))
    return _ffn_call(x, y, mod3, row(emb_ln_g), row(emb_ln_b), w_out[l].astype(_BF16),
                     row(ln1_g[l]), row(ln1_b[l]), w_ffn_in[l].astype(_BF16),
                     w_ffn_out[l].astype(_BF16), row(ln2_g[l]), row(ln2_b[l]))
```

```python
import math

import jax
import jax.numpy as jnp
from jax import lax
from jax.experimental import pallas as pl
from jax.experimental.pallas import tpu as pltpu

D_MODEL = 1024
MIX_A = 512
MIX_B = 512
GMLP_CHUNK = 128
A_GROUPS = 4
HEAD_DIM = 64
HEADS = MIX_B // HEAD_DIM
LORA_W = 64
LORA_A = 64
LORA_G = 160
LORA_G_PAD = 256
RWKV_COLS = 3 * MIX_B + LORA_W + LORA_A + LORA_G
RWKV_COLS_PAD = 3 * MIX_B + LORA_W + LORA_A + LORA_G_PAD
IN_COLS_PAD = 2 * MIX_A + RWKV_COLS_PAD
FFN_HIDDEN = 2816
FFN_CHUNK = 256
DEPTH = 1
ALPHA = (2 * DEPTH) ** 0.25
LN_EPS = 1e-5
GN_EPS = 64e-5
DECAY_SCALE = math.exp(-0.5)

RCHUNK = 64
PAIR = 2 * HEAD_DIM
CUM_BLOCK = 256
MIX_BATCH = 4
SEQ_TILE = 128
FFN_TILE = 512
FFN_SLABS = 2
VMEM_LIMIT_BYTES = 56 * 1024 * 1024

_F32 = jnp.float32
_BF16 = jnp.bfloat16


def _layer_norm(x, g, b, eps):
    mu = jnp.mean(x, axis=-1, keepdims=True)
    d = x - mu
    var = jnp.mean(d * d, axis=-1, keepdims=True)
    return d * lax.rsqrt(var + eps) * g + b


def _gelu_tanh(x):
    c = math.sqrt(2.0 / math.pi)
    return 0.5 * x * (1.0 + jnp.tanh(c * (x + 0.044715 * (x * x * x))))


def _dot(a, b):
    return jnp.dot(a.astype(_BF16), b.astype(_BF16), preferred_element_type=_F32)


def _dot_nt(a, b):
    return lax.dot_general(a.astype(_BF16), b.astype(_BF16), (((1,), (1,)), ((), ())),
                           preferred_element_type=_F32)


def _dot_tn(a, b):
    return lax.dot_general(a.astype(_BF16), b.astype(_BF16), (((0,), (0,)), ((), ())),
                           preferred_element_type=_F32)


def _ada_kernel(c_ref, w_ref, b_ref, o_ref):
    c = c_ref[...]
    cs = c * jax.nn.sigmoid(c)
    o_ref[...] = jnp.dot(cs, w_ref[...], preferred_element_type=_F32,
                         precision=lax.Precision.HIGHEST) + b_ref[...]


def _ada_call(c, w_ada, b_ada):
    B, D = c.shape
    n = w_ada.shape[1]
    tn = 1024
    return pl.pallas_call(
        _ada_kernel,
        grid=(n // tn,),
        in_specs=[pl.BlockSpec((B, D), lambda j: (0, 0)),
                  pl.BlockSpec((D, tn), lambda j: (0, j)),
                  pl.BlockSpec((1, tn), lambda j: (0, j))],
        out_specs=pl.BlockSpec((B, tn), lambda j: (0, j)),
        out_shape=jax.ShapeDtypeStruct((B, n), _F32),
        name="ada_mod",
    )(c, w_ada, b_ada.reshape(1, n))


def _mix_kernel(x_ref, mod_ref, eg_ref, eb_ref, win_ref, mu_ref, sgg_ref, sgb_ref, ws_ref, bs_ref,
                w0_ref, wwa_ref, a0_ref, gup_ref, kk_ref, ka_ref, rk_ref, lg_ref, lb_ref,
                y_ref,
                carry_ref, state_ref, at_ref, rt_ref, bt_ref, kt_ref, v_ref, cum_ref, yb_ref):
    nb, ts = x_ref.shape[0], x_ref.shape[1]
    rows_all = nb * ts
    t_idx = pl.program_id(1)

    @pl.when(t_idx == 0)
    def _():
        carry_ref[...] = jnp.zeros_like(carry_ref)
        state_ref[...] = jnp.zeros_like(state_ref)

    h_parts = []
    for b in range(nb):
        x0 = _layer_norm(x_ref[b], eg_ref[...], eb_ref[...], LN_EPS)
        h_parts.append((x0 * (1.0 + mod_ref[b, 1:2, :]) + mod_ref[b, 0:1, :]).astype(_BF16))
    h = jnp.concatenate(h_parts, axis=0)
    p = jnp.dot(h, win_ref[...], preferred_element_type=_F32)

    u = _gelu_tanh(p[:, 0:MIX_A])
    vg = _layer_norm(_gelu_tanh(p[:, MIX_A:2 * MIX_A]), sgg_ref[...], sgb_ref[...], LN_EPS)
    row = lax.broadcasted_iota(jnp.int32, (GMLP_CHUNK, GMLP_CHUNK), 0)
    col = lax.broadcasted_iota(jnp.int32, (GMLP_CHUNK, GMLP_CHUNK), 1)
    tril = row >= col
    for g in range(A_GROUPS):
        wm = jnp.where(tril, ws_ref[g], 0.0).astype(_BF16)
        cs = slice(g * GMLP_CHUNK, (g + 1) * GMLP_CHUNK)
        for c in range(rows_all // GMLP_CHUNK):
            b, r0 = divmod(c * GMLP_CHUNK, ts)
            rs = slice(c * GMLP_CHUNK, (c + 1) * GMLP_CHUNK)
            s = _dot(wm, vg[rs, cs]) + bs_ref[g]
            y_ref[b, r0:r0 + GMLP_CHUNK, cs] = (u[rs, cs] * s).astype(y_ref.dtype)

    ps = p[:, 2 * MIX_A:]
    prev = pltpu.roll(ps, 1, 0)
    rid = lax.broadcasted_iota(jnp.int32, ps.shape, 0)
    for b in range(nb):
        prev = jnp.where(rid == b * ts, carry_ref[8 * b:8 * b + 1, :], prev)
    for b in range(nb):
        carry_ref[8 * b:8 * b + 1, :] = ps[(b + 1) * ts - 1:(b + 1) * ts, :]
    xs = ps + (prev - ps) * mu_ref[...]

    r = xs[:, 0:MIX_B]
    k = xs[:, MIX_B:2 * MIX_B]
    v = xs[:, 2 * MIX_B:3 * MIX_B]
    wa = xs[:, 3 * MIX_B:3 * MIX_B + 128]
    gd = xs[:, 3 * MIX_B + 128:]

    lane = lax.broadcasted_iota(jnp.int32, wa.shape, 1)
    wa_act = jnp.where(lane < LORA_W, jnp.tanh(wa), wa)
    up = _dot(wa_act, wwa_ref[...])
    sig_w = jax.nn.sigmoid(w0_ref[...] + up[:, 0:MIX_B])
    lr = jax.nn.sigmoid(a0_ref[...] + up[:, MIX_B:])
    gate = _dot(jax.nn.sigmoid(gd), gup_ref[...])

    hr = jnp.right_shift(lax.broadcasted_iota(jnp.int32, (2 * PAIR, 2 * PAIR), 0), 6)
    hc = jnp.right_shift(lax.broadcasted_iota(jnp.int32, (2 * PAIR, 2 * PAIR), 1), 6)
    head_ones = jnp.where(hr == hc, 1.0, 0.0).astype(_BF16)

    def head_sum(a):
        return jnp.concatenate([_dot(a[:, i:i + 2 * PAIR], head_ones)
                                for i in range(0, MIX_B, 2 * PAIR)], axis=1)

    kk = k * kk_ref[...]
    kk = kk * lax.rsqrt(jnp.maximum(head_sum(kk * kk), 1e-24))
    k2 = k * (1.0 + (lr - 1.0) * ka_ref[...])
    bonus = head_sum(r * k2 * rk_ref[...]) * v

    cb = min(CUM_BLOCK, rows_all)
    ri = lax.broadcasted_iota(jnp.int32, (cb, cb), 0)
    ci = lax.broadcasted_iota(jnp.int32, (cb, cb), 1)
    lmat = jnp.where((jnp.right_shift(ri, 6) == jnp.right_shift(ci, 6)) & (ri >= ci),
                     1.0, 0.0).astype(_BF16)
    sw_hi = sig_w.astype(_BF16)
    sw_lo = (sig_w - sw_hi.astype(_F32)).astype(_BF16)
    csum = jnp.concatenate(
        [jnp.dot(lmat, sw_hi[i:i + cb], preferred_element_type=_F32)
         + jnp.dot(lmat, sw_lo[i:i + cb], preferred_element_type=_F32)
         for i in range(0, rows_all, cb)], axis=0)
    cum = -DECAY_SCALE * csum
    e_pos = jnp.exp(cum)
    e_neg = jnp.exp(-cum)
    e_exc = jnp.exp(cum + DECAY_SCALE * sig_w)
    at_ref[...] = (-kk * e_exc).astype(_BF16)
    rt_ref[...] = (r * e_pos).astype(_BF16)
    bt_ref[...] = (kk * lr * e_neg).astype(_BF16)
    kt_ref[...] = (k2 * e_neg).astype(_BF16)
    v_ref[...] = v.astype(_BF16)
    cum_ref[...] = cum

    gi = lax.broadcasted_iota(jnp.int32, (PAIR, PAIR), 0)
    gj = lax.broadcasted_iota(jnp.int32, (PAIR, PAIR), 1)
    gim = jnp.bitwise_and(gi, RCHUNK - 1)
    gjm = jnp.bitwise_and(gj, RCHUNK - 1)
    gmask = (gim > gjm) | ((gi >= RCHUNK) & (gim == gjm))
    bdmask = (gi < RCHUNK) == (gj < RCHUNK)
    left = gj < RCHUNK
    left_c = lax.broadcasted_iota(jnp.int32, (RCHUNK, PAIR), 1) < RCHUNK

    def lsel(a):
        return jnp.where(left if a.shape[0] == PAIR else left_c, a, jnp.zeros_like(a))

    def rsel(a):
        return jnp.where(left if a.shape[0] == PAIR else left_c, jnp.zeros_like(a), a)

    n_pairs = HEADS // 2

    def chunk_body(c, carry):
        lp, r_e, r_o, vp, sbd, wend = [], [], [], [], [], []
        for b in range(nb):
            rows = pl.ds(pl.multiple_of(b * ts + c * RCHUNK, RCHUNK), RCHUNK)
            at_c = at_ref[rows, :]
            rt_c = rt_ref[rows, :]
            bt_c = bt_ref[rows, :]
            kt_c = kt_ref[rows, :]
            v_c = v_ref[rows, :]
            w_end = jnp.exp(cum_ref[pl.ds(b * ts + c * RCHUNK + RCHUNK - 1, 1), :])
            for q in range(n_pairs):
                ls = slice(q * PAIR, (q + 1) * PAIR)
                lp.append(jnp.concatenate([at_c[:, ls], rt_c[:, ls]], axis=0))
                r_e.append(jnp.concatenate([kt_c[:, ls], bt_c[:, ls]], axis=0))
                r_o.append(jnp.concatenate([bt_c[:, ls], kt_c[:, ls]], axis=0))
                vp.append(v_c[:, ls])
                sbd.append(state_ref[b * n_pairs + q])
                wend.append(w_end[:, ls])
        n = range(len(lp))
        g_e = [_dot_nt(lsel(lp[i]), r_e[i]) for i in n]
        g_o = [_dot_nt(rsel(lp[i]), r_o[i]) for i in n]
        ah = [_dot_nt(lp[i], sbd[i]) for i in n]
        gm_e = [jnp.where(gmask, g_e[i], 0.0).astype(_BF16) for i in n]
        gm_o = [jnp.where(gmask, g_o[i], 0.0).astype(_BF16) for i in n]
        vbd = [jnp.concatenate([lsel(vp[i]), rsel(vp[i])], axis=0) for i in n]
        akv = [_dot(jnp.where(left_c, gm_e[i][0:RCHUNK], gm_o[i][0:RCHUNK]), vbd[i]) for i in n]
        x0 = [ah[i][0:RCHUNK] + akv[i] for i in n]
        z = [jnp.concatenate([jnp.where(left_c, gm_o[i][0:RCHUNK].astype(_F32), x0[i]),
                              jnp.where(left_c, x0[i], gm_e[i][0:RCHUNK].astype(_F32))], axis=0)
             for i in n]
        for _ in range(6):
            zb = [z[i].astype(_BF16) for i in n]
            prod = [_dot(jnp.where(bdmask, zb[i], jnp.zeros_like(zb[i])), zb[i]) for i in n]
            z = [prod[i] + jnp.where(bdmask, 0.0, z[i]) for i in n]
        uv = [jnp.concatenate([lsel(vp[i]), lsel(z[i][RCHUNK:].astype(_BF16)),
                               rsel(z[i][0:RCHUNK].astype(_BF16)), rsel(vp[i])], axis=0)
              for i in n]
        y_pairs = [ah[i][RCHUNK:]
                   + _dot(jnp.concatenate([gm_e[i][RCHUNK:], gm_o[i][RCHUNK:]], axis=1), uv[i])
                   for i in n]
        bk = [jnp.concatenate([lsel(r_e[i]), rsel(r_o[i])], axis=0) for i in n]
        new_state = [(sbd[i] + _dot_tn(uv[i], bk[i])) * wend[i] for i in n]
        for i in n:
            state_ref[i] = new_state[i]
        for b in range(nb):
            rows = pl.ds(pl.multiple_of(b * ts + c * RCHUNK, RCHUNK), RCHUNK)
            yb_ref[rows, :] = jnp.concatenate(y_pairs[b * n_pairs:(b + 1) * n_pairs], axis=1)
        return carry

    lax.fori_loop(0, ts // RCHUNK, chunk_body, 0)

    yv = yb_ref[...]
    inv = 1.0 / HEAD_DIM
    mean = head_sum(yv) * inv
    d = yv - mean
    var = head_sum(d * d) * inv
    yn = d * lax.rsqrt(var + GN_EPS) * lg_ref[...] + lb_ref[...]
    yb = ((yn + bonus) * gate).astype(y_ref.dtype)
    for b in range(nb):
        y_ref[b, :, MIX_A:] = yb[b * ts:(b + 1) * ts]


def _mix_call(x, mod3, eg, eb, win_p, mu_p, sgg, sgb, ws, bs_b, w0, wwa, a0, gup_p, kkp, kap, rkp,
              lg, lb):
    B, S, D = x.shape
    nb = math.gcd(B, MIX_BATCH)
    ts = SEQ_TILE
    rows = nb * ts
    const2 = lambda b, t: (0, 0)
    const3 = lambda b, t: (0, 0, 0)

    def full(a):
        return pl.BlockSpec(a.shape, const2 if a.ndim == 2 else const3)

    in_specs = [pl.BlockSpec((nb, ts, D), lambda b, t: (b, t, 0)),
                pl.BlockSpec((nb, 6, D), lambda b, t: (b, 0, 0))]
    consts = [eg, eb, win_p, mu_p, sgg, sgb, ws, bs_b, w0, wwa, a0, gup_p, kkp, kap, rkp, lg, lb]
    in_specs += [full(a) for a in consts]
    scratch = [pltpu.VMEM((8 * nb, RWKV_COLS_PAD), _F32),
               pltpu.VMEM((nb * HEADS // 2, PAIR, PAIR), _F32),
               pltpu.VMEM((rows, MIX_B), _BF16),
               pltpu.VMEM((rows, MIX_B), _BF16),
               pltpu.VMEM((rows, MIX_B), _BF16),
               pltpu.VMEM((rows, MIX_B), _BF16),
               pltpu.VMEM((rows, MIX_B), _BF16),
               pltpu.VMEM((rows, MIX_B), _F32),
               pltpu.VMEM((rows, MIX_B), _F32)]
    return pl.pallas_call(
        _mix_kernel,
        grid=(B // nb, S // ts),
        in_specs=in_specs,
        out_specs=pl.BlockSpec((nb, ts, D), lambda b, t: (b, t, 0)),
        out_shape=jax.ShapeDtypeStruct((B, S, D), _BF16),
        scratch_shapes=scratch,
        compiler_params=pltpu.CompilerParams(
            dimension_semantics=("arbitrary", "arbitrary"),
            vmem_limit_bytes=VMEM_LIMIT_BYTES),
        name="token_mix",
    )(x, mod3, *consts)


def _ffn_kernel(x_ref, y_ref, mod_ref, eg_ref, eb_ref, wout_ref, g1_ref, b1_ref, wfi_ref, wfo_ref,
                g2_ref, b2_ref, o_ref, act_ref):
    gt1 = mod_ref[0, 2:3, :]
    sh2 = mod_ref[0, 3:4, :]
    sc2 = mod_ref[0, 4:5, :]
    gt2 = mod_ref[0, 5:6, :]
    tm = x_ref.shape[1]
    slabs = [slice(i, i + tm // FFN_SLABS) for i in range(0, tm, tm // FFN_SLABS)]

    def pre(s):
        x0 = _layer_norm(x_ref[0, s, :], eg_ref[...], eb_ref[...], LN_EPS)
        mix = jnp.dot(y_ref[0, s, :], wout_ref[...], preferred_element_type=_F32)
        x1 = _layer_norm(ALPHA * x0 + gt1 * mix, g1_ref[...], b1_ref[...], LN_EPS)
        return x1, (x1 * (1.0 + sc2) + sh2).astype(_BF16)

    def hidden(s, h):
        for c in range(FFN_HIDDEN // FFN_CHUNK):
            gcol = pl.ds(c * FFN_CHUNK, FFN_CHUNK)
            ucol = pl.ds(FFN_HIDDEN + c * FFN_CHUNK, FFN_CHUNK)
            gg = jnp.dot(h, wfi_ref[:, gcol], preferred_element_type=_F32)
            uu = jnp.dot(h, wfi_ref[:, ucol], preferred_element_type=_F32)
            act_ref[s, gcol] = (gg * jax.nn.sigmoid(gg) * uu).astype(_BF16)

    def post(s, x1):
        f = jnp.dot(act_ref[s, :], wfo_ref[...], preferred_element_type=_F32)
        o_ref[0, s, :] = _layer_norm(ALPHA * x1 + gt2 * f, g2_ref[...], b2_ref[...], LN_EPS)

    nxt = pre(slabs[0])
    for i, s in enumerate(slabs):
        x1, h = nxt
        if i + 1 < FFN_SLABS:
            nxt = pre(slabs[i + 1])
        hidden(s, h)
        post(s, x1)


def _ffn_call(x, y, mod3, eg, eb, wout, g1, b1, wfi, wfo, g2, b2):
    B, S, D = x.shape
    tm = FFN_TILE
    const2 = lambda b, t: (0, 0)
    tile = lambda b, t: (b, t, 0)

    def resident(a):
        return pl.BlockSpec(a.shape, const2, pipeline_mode=pl.Buffered(1))

    return pl.pallas_call(
        _ffn_kernel,
        grid=(B, S // tm),
        in_specs=[pl.BlockSpec((1, tm, D), tile),
                  pl.BlockSpec((1, tm, D), tile),
                  pl.BlockSpec((1, 6, D), lambda b, t: (b, 0, 0)),
                  resident(eg), resident(eb), resident(wout), resident(g1), resident(b1),
                  resident(wfi), resident(wfo), resident(g2), resident(b2)],
        out_specs=pl.BlockSpec((1, tm, D), tile),
        out_shape=jax.ShapeDtypeStruct((B, S, D), _F32),
        scratch_shapes=[pltpu.VMEM((tm, FFN_HIDDEN), _BF16)],
        compiler_params=pltpu.CompilerParams(
            dimension_semantics=("arbitrary", "arbitrary"),
            vmem_limit_bytes=VMEM_LIMIT_BYTES),
        name="proj_ffn",
    )(x, y, mod3, eg, eb, wout, g1, b1, wfi, wfo, g2, b2)


def kernel(x, c, emb_ln_g, emb_ln_b, w_ada, b_ada, w_in, mu_shift, sg_ln_g, sg_ln_b, w_s, b_s,
           w0, w_up, a0, a_up, g_up, k_k, k_a, r_k, lnx_g, lnx_b, w_out, ln1_g, ln1_b,
           w_ffn_in, w_ffn_out, ln2_g, ln2_b):
    B, S, D = x.shape
    row = lambda a: a.reshape(1, -1).astype(_F32)
    l = 0
    pad_cols = RWKV_COLS_PAD - RWKV_COLS

    win_p = jnp.pad(w_in[l], ((0, 0), (0, pad_cols))).astype(_BF16)
    mu_p = jnp.pad(mu_shift[l], (0, pad_cols)).reshape(1, -1)
    wwa = jnp.zeros((LORA_W + LORA_A, 2 * MIX_B), _F32)
    wwa = wwa.at[:LORA_W, :MIX_B].set(w_up[l]).at[LORA_W:, MIX_B:].set(a_up[l]).astype(_BF16)
    gup_p = jnp.pad(g_up[l], ((0, LORA_G_PAD - LORA_G), (0, 0))).astype(_BF16)
    bs_b = jnp.broadcast_to(b_s[l][:, :, None], (A_GROUPS, GMLP_CHUNK, GMLP_CHUNK))

    mod3 = _ada_call(c, w_ada[l], b_ada[l]).reshape(B, 6, D)
    y = _mix_call(x, mod3, row(emb_ln_g), row(emb_ln_b), win_p, mu_p, row(sg_ln_g[l]),
                  row(sg_ln_b[l]), w_s[l], bs_b, row(w0[l]), wwa, row(a0[l]), gup_p,
                  row(k_k[l]), row(k_a[l]), row(r_k[l]), row(lnx_g[l]), row(lnx_b[l]))
    return _ffn_call(x, y, mod3, row(emb_ln_g), row(emb_ln_b), w_out[l].astype(_BF16),
                     row(ln1_g[l]), row(ln1_b[l]), w_ffn_in[l].astype(_BF16),
                     w_ffn_out[l].astype(_BF16), row(ln2_g[l]), row(ln2_b[l]))
```

```python
import math

import jax
import jax.numpy as jnp
from jax import lax
from jax.experimental import pallas as pl
from jax.experimental.pallas import tpu as pltpu

D_MODEL = 1024
MIX_A = 512
MIX_B = 512
GMLP_CHUNK = 128
A_GROUPS = 4
HEAD_DIM = 64
HEADS = MIX_B // HEAD_DIM
LORA_W = 64
LORA_A = 64
LORA_G = 160
LORA_G_PAD = 256
RWKV_COLS = 3 * MIX_B + LORA_W + LORA_A + LORA_G
RWKV_COLS_PAD = 3 * MIX_B + LORA_W + LORA_A + LORA_G_PAD
IN_COLS_PAD = 2 * MIX_A + RWKV_COLS_PAD
FFN_HIDDEN = 2816
FFN_CHUNK = 256
DEPTH = 1
ALPHA = (2 * DEPTH) ** 0.25
LN_EPS = 1e-5
GN_EPS = 64e-5
DECAY_SCALE = math.exp(-0.5)

RCHUNK = 64
PAIR = 2 * HEAD_DIM
MIX_BATCH = 4
SEQ_TILE = 128
FFN_TILE = 512
FFN_SLABS = 2
VMEM_LIMIT_BYTES = 56 * 1024 * 1024

_F32 = jnp.float32
_BF16 = jnp.bfloat16


def _layer_norm(x, g, b, eps):
    mu = jnp.mean(x, axis=-1, keepdims=True)
    d = x - mu
    var = jnp.mean(d * d, axis=-1, keepdims=True)
    return d * lax.rsqrt(var + eps) * g + b


def _gelu_tanh(x):
    c = math.sqrt(2.0 / math.pi)
    half = 0.5 * x
    return half + half * jnp.tanh(x * (c + (c * 0.044715) * (x * x)))


def _dot(a, b):
    return jnp.dot(a.astype(_BF16), b.astype(_BF16), preferred_element_type=_F32)


def _dot_nt(a, b):
    return lax.dot_general(a.astype(_BF16), b.astype(_BF16), (((1,), (1,)), ((), ())),
                           preferred_element_type=_F32)


def _dot_tn(a, b):
    return lax.dot_general(a.astype(_BF16), b.astype(_BF16), (((0,), (0,)), ((), ())),
                           preferred_element_type=_F32)


def _ada_kernel(c_ref, w_ref, b_ref, o_ref):
    c = c_ref[...]
    cs = c * jax.nn.sigmoid(c)
    o_ref[...] = jnp.dot(cs, w_ref[...], preferred_element_type=_F32,
                         precision=lax.Precision.HIGHEST) + b_ref[...]


def _ada_call(c, w_ada, b_ada):
    B, D = c.shape
    n = w_ada.shape[1]
    tn = 1024
    return pl.pallas_call(
        _ada_kernel,
        grid=(n // tn,),
        in_specs=[pl.BlockSpec((B, D), lambda j: (0, 0)),
                  pl.BlockSpec((D, tn), lambda j: (0, j)),
                  pl.BlockSpec((1, tn), lambda j: (0, j))],
        out_specs=pl.BlockSpec((B, tn), lambda j: (0, j)),
        out_shape=jax.ShapeDtypeStruct((B, n), _F32),
        name="ada_mod",
    )(c, w_ada, b_ada.reshape(1, n))


def _mix_kernel(x_ref, mod_ref, eg_ref, eb_ref, win_ref, mu_ref, sgg_ref, sgb_ref, ws_ref, bs_ref,
                w0_ref, wwa_ref, a0_ref, gup_ref, kk_ref, ka_ref, rk_ref, lg_ref, lb_ref,
                y_ref, carry_ref, state_ref):
    nb, ts = x_ref.shape[0], x_ref.shape[1]
    rows = nb * RCHUNK
    n_pairs = HEADS // 2

    @pl.when(pl.program_id(1) == 0)
    def _():
        carry_ref[...] = jnp.zeros_like(carry_ref)
        state_ref[...] = jnp.zeros_like(state_ref)

    hr = jnp.right_shift(lax.broadcasted_iota(jnp.int32, (2 * PAIR, 2 * PAIR), 0), 6)
    hc = jnp.right_shift(lax.broadcasted_iota(jnp.int32, (2 * PAIR, 2 * PAIR), 1), 6)
    head_ones = jnp.where(hr == hc, 1.0, 0.0).astype(_BF16)

    def head_sum(a):
        return jnp.concatenate([_dot(a[:, i:i + 2 * PAIR], head_ones)
                                for i in range(0, MIX_B, 2 * PAIR)], axis=1)

    row = lax.broadcasted_iota(jnp.int32, (GMLP_CHUNK, GMLP_CHUNK), 0)
    col = lax.broadcasted_iota(jnp.int32, (GMLP_CHUNK, GMLP_CHUNK), 1)
    wm = [jnp.where(row >= col, ws_ref[g], 0.0).astype(_BF16) for g in range(A_GROUPS)]

    ri = lax.broadcasted_iota(jnp.int32, (rows, rows), 0)
    ci = lax.broadcasted_iota(jnp.int32, (rows, rows), 1)
    lmat = jnp.where((jnp.right_shift(ri, 6) == jnp.right_shift(ci, 6)) & (ri >= ci),
                     1.0, 0.0).astype(_BF16)

    gi = lax.broadcasted_iota(jnp.int32, (PAIR, PAIR), 0)
    gj = lax.broadcasted_iota(jnp.int32, (PAIR, PAIR), 1)
    gim = jnp.bitwise_and(gi, RCHUNK - 1)
    gjm = jnp.bitwise_and(gj, RCHUNK - 1)
    gmask = (gim > gjm) | ((gi >= RCHUNK) & (gim == gjm))
    bdmask = (gi < RCHUNK) == (gj < RCHUNK)
    left = gj < RCHUNK
    left_c = lax.broadcasted_iota(jnp.int32, (RCHUNK, PAIR), 1) < RCHUNK

    def lsel(a):
        return jnp.where(left if a.shape[0] == PAIR else left_c, a, jnp.zeros_like(a))

    def rsel(a):
        return jnp.where(left if a.shape[0] == PAIR else left_c, jnp.zeros_like(a), a)

    def prepare(c, vg_prev):
        tok = slice(c * RCHUNK, (c + 1) * RCHUNK)
        h_parts = []
        for b in range(nb):
            x0 = _layer_norm(x_ref[b, tok, :], eg_ref[...], eb_ref[...], LN_EPS)
            h_parts.append((x0 * (1.0 + mod_ref[b, 1:2, :]) + mod_ref[b, 0:1, :]).astype(_BF16))
        h = jnp.concatenate(h_parts, axis=0)
        p = jnp.dot(h, win_ref[...], preferred_element_type=_F32)

        u = _gelu_tanh(p[:, 0:MIX_A])
        vg = _layer_norm(_gelu_tanh(p[:, MIX_A:2 * MIX_A]), sgg_ref[...], sgb_ref[...],
                         LN_EPS).astype(_BF16)
        second = c % 2 == 1
        hrow = slice(RCHUNK, 2 * RCHUNK) if second else slice(0, RCHUNK)
        for g in range(A_GROUPS):
            cs = slice(g * GMLP_CHUNK, (g + 1) * GMLP_CHUNK)
            for b in range(nb):
                rs = slice(b * RCHUNK, (b + 1) * RCHUNK)
                if second:
                    s = jnp.dot(wm[g][hrow, :],
                                jnp.concatenate([vg_prev[rs, cs], vg[rs, cs]], axis=0),
                                preferred_element_type=_F32)
                else:
                    s = jnp.dot(wm[g][hrow, 0:RCHUNK], vg[rs, cs], preferred_element_type=_F32)
                s = s + bs_ref[g, hrow, :]
                y_ref[b, tok, cs] = (u[rs, cs] * s).astype(y_ref.dtype)

        ps = p[:, 2 * MIX_A:]
        prev = pltpu.roll(ps, 1, 0)
        rid = lax.broadcasted_iota(jnp.int32, ps.shape, 0)
        for b in range(nb):
            prev = jnp.where(rid == b * RCHUNK, carry_ref[8 * b:8 * b + 1, :], prev)
        for b in range(nb):
            carry_ref[8 * b:8 * b + 1, :] = ps[(b + 1) * RCHUNK - 1:(b + 1) * RCHUNK, :]
        xs = ps + (prev - ps) * mu_ref[...]

        r = xs[:, 0:MIX_B]
        k = xs[:, MIX_B:2 * MIX_B]
        v = xs[:, 2 * MIX_B:3 * MIX_B]
        wa = xs[:, 3 * MIX_B:3 * MIX_B + 128]
        gd = xs[:, 3 * MIX_B + 128:]

        lane = lax.broadcasted_iota(jnp.int32, wa.shape, 1)
        wa_act = jnp.where(lane < LORA_W, jnp.tanh(wa), wa)
        up = _dot(wa_act, wwa_ref[...])
        sig_w = jax.nn.sigmoid(w0_ref[...] + up[:, 0:MIX_B])
        lr = jax.nn.sigmoid(a0_ref[...] + up[:, MIX_B:])
        gate = _dot(jax.nn.sigmoid(gd), gup_ref[...])

        kk = k * kk_ref[...]
        kk = kk * lax.rsqrt(jnp.maximum(head_sum(kk * kk), 1e-24))
        k2 = k * (1.0 + (lr - 1.0) * ka_ref[...])
        bonus = head_sum(r * k2 * rk_ref[...]) * v

        sw_hi = sig_w.astype(_BF16)
        sw_lo = (sig_w - sw_hi.astype(_F32)).astype(_BF16)
        csum = (jnp.dot(lmat, sw_hi, preferred_element_type=_F32)
                + jnp.dot(lmat, sw_lo, preferred_element_type=_F32))
        cum = -DECAY_SCALE * csum
        e_pos = jnp.exp(cum)
        e_neg = jnp.exp(-cum)
        e_exc = jnp.exp(cum + DECAY_SCALE * sig_w)
        ops = dict(at=(-kk * e_exc).astype(_BF16), rt=(r * e_pos).astype(_BF16),
                   bt=(kk * lr * e_neg).astype(_BF16), kt=(k2 * e_neg).astype(_BF16),
                   v=v.astype(_BF16), e_pos=e_pos, bonus=bonus, gate=gate)
        return ops, vg

    def recur(ops, state):
        lp, r_e, r_o, vp, wend = [], [], [], [], []
        for b in range(nb):
            rs = slice(b * RCHUNK, (b + 1) * RCHUNK)
            last = (b + 1) * RCHUNK - 1
            for q in range(n_pairs):
                ls = slice(q * PAIR, (q + 1) * PAIR)
                lp.append(jnp.concatenate([ops["at"][rs, ls], ops["rt"][rs, ls]], axis=0))
                r_e.append(jnp.concatenate([ops["kt"][rs, ls], ops["bt"][rs, ls]], axis=0))
                r_o.append(jnp.concatenate([ops["bt"][rs, ls], ops["kt"][rs, ls]], axis=0))
                vp.append(ops["v"][rs, ls])
                wend.append(ops["e_pos"][last:last + 1, ls])
        n = range(len(lp))
        g_e = [_dot_nt(lsel(lp[i]), r_e[i]) for i in n]
        g_o = [_dot_nt(rsel(lp[i]), r_o[i]) for i in n]
        ah = [_dot_nt(lp[i], state[i]) for i in n]
        gm_e = [jnp.where(gmask, g_e[i], 0.0).astype(_BF16) for i in n]
        gm_o = [jnp.where(gmask, g_o[i], 0.0).astype(_BF16) for i in n]
        vbd = [jnp.concatenate([lsel(vp[i]), rsel(vp[i])], axis=0) for i in n]
        akv = [_dot(jnp.where(left_c, gm_e[i][0:RCHUNK], gm_o[i][0:RCHUNK]), vbd[i]) for i in n]
        x0 = [ah[i][0:RCHUNK] + akv[i] for i in n]
        z = [jnp.concatenate([jnp.where(left_c, gm_o[i][0:RCHUNK].astype(_F32), x0[i]),
                              jnp.where(left_c, x0[i], gm_e[i][0:RCHUNK].astype(_F32))], axis=0)
             for i in n]
        for _ in range(6):
            zb = [z[i].astype(_BF16) for i in n]
            prod = [_dot(jnp.where(bdmask, zb[i], jnp.zeros_like(zb[i])), zb[i]) for i in n]
            z = [prod[i] + jnp.where(bdmask, 0.0, z[i]) for i in n]
        uv = [jnp.concatenate([lsel(vp[i]), lsel(z[i][RCHUNK:].astype(_BF16)),
                               rsel(z[i][0:RCHUNK].astype(_BF16)), rsel(vp[i])], axis=0)
              for i in n]
        y_pairs = [ah[i][RCHUNK:]
                   + _dot(jnp.concatenate([gm_e[i][RCHUNK:], gm_o[i][RCHUNK:]], axis=1), uv[i])
                   for i in n]
        bk = [jnp.concatenate([lsel(r_e[i]), rsel(r_o[i])], axis=0) for i in n]
        new_state = [(state[i] + _dot_tn(uv[i], bk[i])) * wend[i] for i in n]
        yv = jnp.concatenate([jnp.concatenate(y_pairs[b * n_pairs:(b + 1) * n_pairs], axis=1)
                              for b in range(nb)], axis=0)
        return yv, new_state

    def finish(c, yv, ops):
        inv = 1.0 / HEAD_DIM
        mean = head_sum(yv) * inv
        d = yv - mean
        var = head_sum(d * d) * inv
        yn = d * lax.rsqrt(var + GN_EPS) * lg_ref[...] + lb_ref[...]
        yb = ((yn + ops["bonus"]) * ops["gate"]).astype(y_ref.dtype)
        for b in range(nb):
            y_ref[b, c * RCHUNK:(c + 1) * RCHUNK, MIX_A:] = yb[b * RCHUNK:(b + 1) * RCHUNK]

    state = [state_ref[i] for i in range(nb * n_pairs)]
    vg_prev = None
    for c in range(ts // RCHUNK):
        ops, vg_prev = prepare(c, vg_prev)
        yv, state = recur(ops, state)
        finish(c, yv, ops)
    for i in range(nb * n_pairs):
        state_ref[i] = state[i]


def _mix_call(x, mod3, eg, eb, win_p, mu_p, sgg, sgb, ws, bs_b, w0, wwa, a0, gup_p, kkp, kap, rkp,
              lg, lb):
    B, S, D = x.shape
    nb = math.gcd(B, MIX_BATCH)
    ts = SEQ_TILE
    assert ts % GMLP_CHUNK == 0 and S % ts == 0
    const2 = lambda b, t: (0, 0)
    const3 = lambda b, t: (0, 0, 0)

    def full(a):
        return pl.BlockSpec(a.shape, const2 if a.ndim == 2 else const3)

    in_specs = [pl.BlockSpec((nb, ts, D), lambda b, t: (b, t, 0)),
                pl.BlockSpec((nb, 6, D), lambda b, t: (b, 0, 0))]
    consts = [eg, eb, win_p, mu_p, sgg, sgb, ws, bs_b, w0, wwa, a0, gup_p, kkp, kap, rkp, lg, lb]
    in_specs += [full(a) for a in consts]
    scratch = [pltpu.VMEM((8 * nb, RWKV_COLS_PAD), _F32),
               pltpu.VMEM((nb * HEADS // 2, PAIR, PAIR), _F32)]
    return pl.pallas_call(
        _mix_kernel,
        grid=(B // nb, S // ts),
        in_specs=in_specs,
        out_specs=pl.BlockSpec((nb, ts, D), lambda b, t: (b, t, 0)),
        out_shape=jax.ShapeDtypeStruct((B, S, D), _BF16),
        scratch_shapes=scratch,
        compiler_params=pltpu.CompilerParams(
            dimension_semantics=("arbitrary", "arbitrary"),
            vmem_limit_bytes=VMEM_LIMIT_BYTES),
        name="token_mix",
    )(x, mod3, *consts)


def _ffn_kernel(x_ref, y_ref, mod_ref, eg_ref, eb_ref, wout_ref, g1_ref, b1_ref, wfi_ref, wfo_ref,
                g2_ref, b2_ref, o_ref, act_ref):
    gt1 = mod_ref[0, 2:3, :]
    sh2 = mod_ref[0, 3:4, :]
    sc2 = mod_ref[0, 4:5, :]
    gt2 = mod_ref[0, 5:6, :]
    tm = x_ref.shape[1]
    slabs = [slice(i, i + tm // FFN_SLABS) for i in range(0, tm, tm // FFN_SLABS)]

    def pre(s):
        x0 = _layer_norm(x_ref[0, s, :], eg_ref[...], eb_ref[...], LN_EPS)
        mix = jnp.dot(y_ref[0, s, :], wout_ref[...], preferred_element_type=_F32)
        x1 = _layer_norm(ALPHA * x0 + gt1 * mix, g1_ref[...], b1_ref[...], LN_EPS)
        return x1, (x1 * (1.0 + sc2) + sh2).astype(_BF16)

    def hidden(s, h):
        for c in range(FFN_HIDDEN // FFN_CHUNK):
            gcol = pl.ds(c * FFN_CHUNK, FFN_CHUNK)
            ucol = pl.ds(FFN_HIDDEN + c * FFN_CHUNK, FFN_CHUNK)
            gg = jnp.dot(h, wfi_ref[:, gcol], preferred_element_type=_F32)
            uu = jnp.dot(h, wfi_ref[:, ucol], preferred_element_type=_F32)
            act_ref[s, gcol] = (gg * jax.nn.sigmoid(gg) * uu).astype(_BF16)

    def post(s, x1):
        f = jnp.dot(act_ref[s, :], wfo_ref[...], preferred_element_type=_F32)
        o_ref[0, s, :] = _layer_norm(ALPHA * x1 + gt2 * f, g2_ref[...], b2_ref[...], LN_EPS)

    nxt = pre(slabs[0])
    for i, s in enumerate(slabs):
        x1, h = nxt
        if i + 1 < FFN_SLABS:
            nxt = pre(slabs[i + 1])
        hidden(s, h)
        post(s, x1)


def _ffn_call(x, y, mod3, eg, eb, wout, g1, b1, wfi, wfo, g2, b2):
    B, S, D = x.shape
    tm = FFN_TILE
    const2 = lambda b, t: (0, 0)
    tile = lambda b, t: (b, t, 0)

    def resident(a):
        return pl.BlockSpec(a.shape, const2, pipeline_mode=pl.Buffered(1))

    return pl.pallas_call(
        _ffn_kernel,
        grid=(B, S // tm),
        in_specs=[pl.BlockSpec((1, tm, D), tile),
                  pl.BlockSpec((1, tm, D), tile),
                  pl.BlockSpec((1, 6, D), lambda b, t: (b, 0, 0)),
                  resident(eg), resident(eb), resident(wout), resident(g1), resident(b1),
                  resident(wfi), resident(wfo), resident(g2), resident(b2)],
        out_specs=pl.BlockSpec((1, tm, D), tile),
        out_shape=jax.ShapeDtypeStruct((B, S, D), _F32),
        scratch_shapes=[pltpu.VMEM((tm, FFN_HIDDEN), _BF16)],
        compiler_params=pltpu.CompilerParams(
            dimension_semantics=("arbitrary", "arbitrary"),
            vmem_limit_bytes=VMEM_LIMIT_BYTES),
        name="proj_ffn",
    )(x, y, mod3, eg, eb, wout, g1, b1, wfi, wfo, g2, b2)


def kernel(x, c, emb_ln_g, emb_ln_b, w_ada, b_ada, w_in, mu_shift, sg_ln_g, sg_ln_b, w_s, b_s,
           w0, w_up, a0, a_up, g_up, k_k, k_a, r_k, lnx_g, lnx_b, w_out, ln1_g, ln1_b,
           w_ffn_in, w_ffn_out, ln2_g, ln2_b):
    B, S, D = x.shape
    row = lambda a: a.reshape(1, -1).astype(_F32)
    l = 0
    pad_cols = RWKV_COLS_PAD - RWKV_COLS

    win_p = jnp.pad(w_in[l], ((0, 0), (0, pad_cols))).astype(_BF16)
    mu_p = jnp.pad(mu_shift[l], (0, pad_cols)).reshape(1, -1)
    wwa = jnp.zeros((LORA_W + LORA_A, 2 * MIX_B), _F32)
    wwa = wwa.at[:LORA_W, :MIX_B].set(w_up[l]).at[LORA_W:, MIX_B:].set(a_up[l]).astype(_BF16)
    gup_p = jnp.pad(g_up[l], ((0, LORA_G_PAD - LORA_G), (0, 0))).astype(_BF16)
    bs_b = jnp.broadcast_to(b_s[l][:, :, None], (A_GROUPS, GMLP_CHUNK, GMLP_CHUNK))

    mod3 = _ada_call(c, w_ada[l], b_ada[l]).reshape(B, 6, D)
    y = _mix_call(x, mod3, row(emb_ln_g), row(emb_ln_b), win_p, mu_p, row(sg_ln_g[l]),
                  row(sg_ln_b[l]), w_s[l], bs_b, row(w0[l]), wwa, row(a0[l]), gup_p,
                  row(k_k[l]), row(k_a[l]), row(r_k[l]), row(lnx_g[l]), row(lnx_b[l]))
    return _ffn_call(x, y, mod3, row(emb_ln_g), row(emb_ln_b), w_out[l].astype(_BF16),
                     row(ln1_g[l]), row(ln1_b[l]), w_ffn_in[l].astype(_BF16),
                     w_ffn_out[l].astype(_BF16), row(ln2_g[l]), row(ln2_b[l]))
```

```python
import math

import jax
import jax.numpy as jnp
from jax import lax
from jax.experimental import pallas as pl
from jax.experimental.pallas import tpu as pltpu

D_MODEL = 1024
MIX_A = 512
MIX_B = 512
GMLP_CHUNK = 128
A_GROUPS = 4
HEAD_DIM = 64
HEADS = MIX_B // HEAD_DIM
LORA_W = 64
LORA_A = 64
LORA_G = 160
LORA_G_PAD = 256
RWKV_COLS = 3 * MIX_B + LORA_W + LORA_A + LORA_G
RWKV_COLS_PAD = 3 * MIX_B + LORA_W + LORA_A + LORA_G_PAD
IN_COLS_PAD = 2 * MIX_A + RWKV_COLS_PAD
FFN_HIDDEN = 2816
FFN_CHUNK = 256
DEPTH = 1
ALPHA = (2 * DEPTH) ** 0.25
LN_EPS = 1e-5
GN_EPS = 64e-5
DECAY_SCALE = math.exp(-0.5)

RCHUNK = 64
PAIR = 2 * HEAD_DIM
MIX_BATCH = 4
SEQ_TILE = 128
FFN_TILE = 512
FFN_SLABS = 2
VMEM_LIMIT_BYTES = 56 * 1024 * 1024

_F32 = jnp.float32
_BF16 = jnp.bfloat16


def _layer_norm(x, g, b, eps):
    mu = jnp.mean(x, axis=-1, keepdims=True)
    d = x - mu
    var = jnp.mean(d * d, axis=-1, keepdims=True)
    return d * lax.rsqrt(var + eps) * g + b


def _gelu_tanh(x):
    c = math.sqrt(2.0 / math.pi)
    half = 0.5 * x
    return half + half * jnp.tanh(x * (c + (c * 0.044715) * (x * x)))


def _dot(a, b):
    return jnp.dot(a.astype(_BF16), b.astype(_BF16), preferred_element_type=_F32)


def _dot_nt(a, b):
    return lax.dot_general(a.astype(_BF16), b.astype(_BF16), (((1,), (1,)), ((), ())),
                           preferred_element_type=_F32)


def _dot_tn(a, b):
    return lax.dot_general(a.astype(_BF16), b.astype(_BF16), (((0,), (0,)), ((), ())),
                           preferred_element_type=_F32)


def _ada_kernel(c_ref, w_ref, b_ref, o_ref):
    c = c_ref[...]
    cs = c * jax.nn.sigmoid(c)
    o_ref[...] = jnp.dot(cs, w_ref[...], preferred_element_type=_F32,
                         precision=lax.Precision.HIGHEST) + b_ref[...]


def _ada_call(c, w_ada, b_ada):
    B, D = c.shape
    n = w_ada.shape[1]
    tn = 1024
    return pl.pallas_call(
        _ada_kernel,
        grid=(n // tn,),
        in_specs=[pl.BlockSpec((B, D), lambda j: (0, 0)),
                  pl.BlockSpec((D, tn), lambda j: (0, j)),
                  pl.BlockSpec((1, tn), lambda j: (0, j))],
        out_specs=pl.BlockSpec((B, tn), lambda j: (0, j)),
        out_shape=jax.ShapeDtypeStruct((B, n), _F32),
        name="ada_mod",
    )(c, w_ada, b_ada.reshape(1, n))


def _mix_kernel(x_ref, mod_ref, eg_ref, eb_ref, win_ref, mu_ref, sgg_ref, sgb_ref, ws_ref, bs_ref,
                w0_ref, wwa_ref, a0_ref, gup_ref, kk_ref, ka_ref, rk_ref, lg_ref, lb_ref,
                y_ref, carry_ref, state_ref):
    nb, ts = x_ref.shape[0], x_ref.shape[1]
    rows = nb * RCHUNK
    n_pairs = HEADS // 2

    @pl.when(pl.program_id(1) == 0)
    def _():
        carry_ref[...] = jnp.zeros_like(carry_ref)
        state_ref[...] = jnp.zeros_like(state_ref)

    hr = jnp.right_shift(lax.broadcasted_iota(jnp.int32, (2 * PAIR, 2 * PAIR), 0), 6)
    hc = jnp.right_shift(lax.broadcasted_iota(jnp.int32, (2 * PAIR, 2 * PAIR), 1), 6)
    head_ones = jnp.where(hr == hc, 1.0, 0.0).astype(_BF16)

    def head_sums(*arrays):
        halves = [a[:, i:i + 2 * PAIR] for a in arrays for i in range(0, MIX_B, 2 * PAIR)]
        out = _dot(jnp.concatenate(halves, axis=0), head_ones)
        m = arrays[0].shape[0]
        parts = [out[j * m:(j + 1) * m] for j in range(len(halves))]
        per = MIX_B // (2 * PAIR)
        return [jnp.concatenate(parts[j * per:(j + 1) * per], axis=1)
                for j in range(len(arrays))]

    def head_sum(a):
        return head_sums(a)[0]

    row = lax.broadcasted_iota(jnp.int32, (GMLP_CHUNK, GMLP_CHUNK), 0)
    col = lax.broadcasted_iota(jnp.int32, (GMLP_CHUNK, GMLP_CHUNK), 1)
    wm = [jnp.where(row >= col, ws_ref[g], 0.0).astype(_BF16) for g in range(A_GROUPS)]

    ri = lax.broadcasted_iota(jnp.int32, (rows, rows), 0)
    ci = lax.broadcasted_iota(jnp.int32, (rows, rows), 1)
    lmat = jnp.where((jnp.right_shift(ri, 6) == jnp.right_shift(ci, 6)) & (ri >= ci),
                     1.0, 0.0).astype(_BF16)

    gi = lax.broadcasted_iota(jnp.int32, (PAIR, PAIR), 0)
    gj = lax.broadcasted_iota(jnp.int32, (PAIR, PAIR), 1)
    gim = jnp.bitwise_and(gi, RCHUNK - 1)
    gjm = jnp.bitwise_and(gj, RCHUNK - 1)
    gmask = (gim > gjm) | ((gi >= RCHUNK) & (gim == gjm))
    bdmask = (gi < RCHUNK) == (gj < RCHUNK)
    left = gj < RCHUNK
    left_c = lax.broadcasted_iota(jnp.int32, (RCHUNK, PAIR), 1) < RCHUNK

    def lsel(a):
        return jnp.where(left if a.shape[0] == PAIR else left_c, a, jnp.zeros_like(a))

    def rsel(a):
        return jnp.where(left if a.shape[0] == PAIR else left_c, jnp.zeros_like(a), a)

    def prepare(c, vg_prev):
        tok = slice(c * RCHUNK, (c + 1) * RCHUNK)
        h_parts = []
        for b in range(nb):
            x0 = _layer_norm(x_ref[b, tok, :], eg_ref[...], eb_ref[...], LN_EPS)
            h_parts.append((x0 * (1.0 + mod_ref[b, 1:2, :]) + mod_ref[b, 0:1, :]).astype(_BF16))
        h = jnp.concatenate(h_parts, axis=0)
        p = jnp.dot(h, win_ref[...], preferred_element_type=_F32)

        u = _gelu_tanh(p[:, 0:MIX_A])
        vg = _layer_norm(_gelu_tanh(p[:, MIX_A:2 * MIX_A]), sgg_ref[...], sgb_ref[...],
                         LN_EPS).astype(_BF16)
        second = c % 2 == 1
        hrow = slice(RCHUNK, 2 * RCHUNK) if second else slice(0, RCHUNK)
        bslice = [slice(b * RCHUNK, (b + 1) * RCHUNK) for b in range(nb)]
        for g in range(A_GROUPS):
            cs = slice(g * GMLP_CHUNK, (g + 1) * GMLP_CHUNK)
            if second:
                vcat = jnp.concatenate(
                    [jnp.concatenate([vg_prev[rs, cs], vg[rs, cs]], axis=0) for rs in bslice],
                    axis=1)
                s_all = jnp.dot(wm[g][hrow, :], vcat, preferred_element_type=_F32)
            else:
                vcat = jnp.concatenate([vg[rs, cs] for rs in bslice], axis=1)
                s_all = jnp.dot(wm[g][hrow, 0:RCHUNK], vcat, preferred_element_type=_F32)
            for b, rs in enumerate(bslice):
                s = s_all[:, b * GMLP_CHUNK:(b + 1) * GMLP_CHUNK] + bs_ref[g, hrow, :]
                y_ref[b, tok, cs] = (u[rs, cs] * s).astype(y_ref.dtype)

        ps = p[:, 2 * MIX_A:]
        prev = pltpu.roll(ps, 1, 0)
        rid = lax.broadcasted_iota(jnp.int32, ps.shape, 0)
        for b in range(nb):
            prev = jnp.where(rid == b * RCHUNK, carry_ref[8 * b:8 * b + 1, :], prev)
        for b in range(nb):
            carry_ref[8 * b:8 * b + 1, :] = ps[(b + 1) * RCHUNK - 1:(b + 1) * RCHUNK, :]
        xs = ps + (prev - ps) * mu_ref[...]

        r = xs[:, 0:MIX_B]
        k = xs[:, MIX_B:2 * MIX_B]
        v = xs[:, 2 * MIX_B:3 * MIX_B]
        wa = xs[:, 3 * MIX_B:3 * MIX_B + 128]
        gd = xs[:, 3 * MIX_B + 128:]

        lane = lax.broadcasted_iota(jnp.int32, wa.shape, 1)
        wa_act = jnp.where(lane < LORA_W, jnp.tanh(wa), wa)
        up = _dot(wa_act, wwa_ref[...])
        sig_w = jax.nn.sigmoid(w0_ref[...] + up[:, 0:MIX_B])
        lr = jax.nn.sigmoid(a0_ref[...] + up[:, MIX_B:])
        gate = _dot(jax.nn.sigmoid(gd), gup_ref[...])

        kk = k * kk_ref[...]
        k2 = k * (1.0 + (lr - 1.0) * ka_ref[...])
        kk_ss, rk_sum = head_sums(kk * kk, r * k2 * rk_ref[...])
        kk = kk * lax.rsqrt(jnp.maximum(kk_ss, 1e-24))
        bonus = rk_sum * v

        sw_hi = sig_w.astype(_BF16)
        sw_lo = (sig_w - sw_hi.astype(_F32)).astype(_BF16)
        csum2 = jnp.dot(lmat, jnp.concatenate([sw_hi, sw_lo], axis=1),
                        preferred_element_type=_F32)
        csum = csum2[:, 0:MIX_B] + csum2[:, MIX_B:]
        cum = -DECAY_SCALE * csum
        e_pos = jnp.exp(cum)
        e_neg = jnp.exp(-cum)
        e_exc = jnp.exp(cum + DECAY_SCALE * sig_w)
        ops = dict(at=(-kk * e_exc).astype(_BF16), rt=(r * e_pos).astype(_BF16),
                   bt=(kk * lr * e_neg).astype(_BF16), kt=(k2 * e_neg).astype(_BF16),
                   v=v.astype(_BF16), e_pos=e_pos, bonus=bonus, gate=gate)
        return ops, vg

    def recur(ops, state):
        lp, r_e, r_o, vp, wend = [], [], [], [], []
        for b in range(nb):
            rs = slice(b * RCHUNK, (b + 1) * RCHUNK)
            last = (b + 1) * RCHUNK - 1
            for q in range(n_pairs):
                ls = slice(q * PAIR, (q + 1) * PAIR)
                lp.append(jnp.concatenate([ops["at"][rs, ls], ops["rt"][rs, ls]], axis=0))
                r_e.append(jnp.concatenate([ops["kt"][rs, ls], ops["bt"][rs, ls]], axis=0))
                r_o.append(jnp.concatenate([ops["bt"][rs, ls], ops["kt"][rs, ls]], axis=0))
                vp.append(ops["v"][rs, ls])
                wend.append(ops["e_pos"][last:last + 1, ls])
        n = range(len(lp))
        bk = [jnp.concatenate([lsel(r_e[i]), rsel(r_o[i])], axis=0) for i in n]
        gah = [_dot_nt(lp[i], jnp.concatenate([bk[i], state[i].astype(_BF16)], axis=0))
               for i in n]
        ah = [gah[i][:, 2 * PAIR:] for i in n]
        gm_e = [jnp.where(gmask, gah[i][:, 0:PAIR], 0.0).astype(_BF16) for i in n]
        gm_o = [jnp.where(gmask, gah[i][:, PAIR:2 * PAIR], 0.0).astype(_BF16) for i in n]
        vbd = [jnp.concatenate([lsel(vp[i]), rsel(vp[i])], axis=0) for i in n]
        akv = [_dot(jnp.where(left_c, gm_e[i][0:RCHUNK], gm_o[i][0:RCHUNK]), vbd[i]) for i in n]
        x0 = [ah[i][0:RCHUNK] + akv[i] for i in n]
        z = [jnp.concatenate([jnp.where(left_c, gm_o[i][0:RCHUNK].astype(_F32), x0[i]),
                              jnp.where(left_c, x0[i], gm_e[i][0:RCHUNK].astype(_F32))], axis=0)
             for i in n]
        for _ in range(6):
            zb = [z[i].astype(_BF16) for i in n]
            prod = [_dot(jnp.where(bdmask, zb[i], jnp.zeros_like(zb[i])), zb[i]) for i in n]
            z = [prod[i] + jnp.where(bdmask, 0.0, z[i]) for i in n]
        uv = [jnp.concatenate([lsel(vp[i]), lsel(z[i][RCHUNK:].astype(_BF16)),
                               rsel(z[i][0:RCHUNK].astype(_BF16)), rsel(vp[i])], axis=0)
              for i in n]
        y_pairs = [ah[i][RCHUNK:]
                   + _dot(jnp.concatenate([gm_e[i][RCHUNK:], gm_o[i][RCHUNK:]], axis=1), uv[i])
                   for i in n]
        new_state = [(state[i] + _dot_tn(uv[i], bk[i])) * wend[i] for i in n]
        yv = jnp.concatenate([jnp.concatenate(y_pairs[b * n_pairs:(b + 1) * n_pairs], axis=1)
                              for b in range(nb)], axis=0)
        return yv, new_state

    def finish(c, yv, ops):
        inv = 1.0 / HEAD_DIM
        mean = head_sum(yv) * inv
        d = yv - mean
        var = head_sum(d * d) * inv
        yn = d * lax.rsqrt(var + GN_EPS) * lg_ref[...] + lb_ref[...]
        yb = ((yn + ops["bonus"]) * ops["gate"]).astype(y_ref.dtype)
        for b in range(nb):
            y_ref[b, c * RCHUNK:(c + 1) * RCHUNK, MIX_A:] = yb[b * RCHUNK:(b + 1) * RCHUNK]

    state = [state_ref[i] for i in range(nb * n_pairs)]
    vg_prev = None
    for c in range(ts // RCHUNK):
        ops, vg_prev = prepare(c, vg_prev)
        yv, state = recur(ops, state)
        finish(c, yv, ops)
    for i in range(nb * n_pairs):
        state_ref[i] = state[i]


def _mix_call(x, mod3, eg, eb, win_p, mu_p, sgg, sgb, ws, bs_b, w0, wwa, a0, gup_p, kkp, kap, rkp,
              lg, lb):
    B, S, D = x.shape
    nb = math.gcd(B, MIX_BATCH)
    ts = SEQ_TILE
    assert ts % GMLP_CHUNK == 0 and S % ts == 0
    const2 = lambda b, t: (0, 0)
    const3 = lambda b, t: (0, 0, 0)

    def full(a):
        return pl.BlockSpec(a.shape, const2 if a.ndim == 2 else const3)

    in_specs = [pl.BlockSpec((nb, ts, D), lambda b, t: (b, t, 0)),
                pl.BlockSpec((nb, 6, D), lambda b, t: (b, 0, 0))]
    consts = [eg, eb, win_p, mu_p, sgg, sgb, ws, bs_b, w0, wwa, a0, gup_p, kkp, kap, rkp, lg, lb]
    in_specs += [full(a) for a in consts]
    scratch = [pltpu.VMEM((8 * nb, RWKV_COLS_PAD), _F32),
               pltpu.VMEM((nb * HEADS // 2, PAIR, PAIR), _F32)]
    return pl.pallas_call(
        _mix_kernel,
        grid=(B // nb, S // ts),
        in_specs=in_specs,
        out_specs=pl.BlockSpec((nb, ts, D), lambda b, t: (b, t, 0)),
        out_shape=jax.ShapeDtypeStruct((B, S, D), _BF16),
        scratch_shapes=scratch,
        compiler_params=pltpu.CompilerParams(
            dimension_semantics=("arbitrary", "arbitrary"),
            vmem_limit_bytes=VMEM_LIMIT_BYTES),
        name="token_mix",
    )(x, mod3, *consts)


def _ffn_kernel(x_ref, y_ref, mod_ref, eg_ref, eb_ref, wout_ref, g1_ref, b1_ref, wfi_ref, wfo_ref,
                g2_ref, b2_ref, o_ref, act_ref):
    gt1 = mod_ref[0, 2:3, :]
    sh2 = mod_ref[0, 3:4, :]
    sc2 = mod_ref[0, 4:5, :]
    gt2 = mod_ref[0, 5:6, :]
    tm = x_ref.shape[1]
    slabs = [slice(i, i + tm // FFN_SLABS) for i in range(0, tm, tm // FFN_SLABS)]

    def pre(s):
        x0 = _layer_norm(x_ref[0, s, :], eg_ref[...], eb_ref[...], LN_EPS)
        mix = jnp.dot(y_ref[0, s, :], wout_ref[...], preferred_element_type=_F32)
        x1 = _layer_norm(ALPHA * x0 + gt1 * mix, g1_ref[...], b1_ref[...], LN_EPS)
        return x1, (x1 * (1.0 + sc2) + sh2).astype(_BF16)

    def hidden(s, h):
        for c in range(FFN_HIDDEN // FFN_CHUNK):
            gcol = pl.ds(c * FFN_CHUNK, FFN_CHUNK)
            ucol = pl.ds(FFN_HIDDEN + c * FFN_CHUNK, FFN_CHUNK)
            gg = jnp.dot(h, wfi_ref[:, gcol], preferred_element_type=_F32)
            uu = jnp.dot(h, wfi_ref[:, ucol], preferred_element_type=_F32)
            act_ref[s, gcol] = (gg * jax.nn.sigmoid(gg) * uu).astype(_BF16)

    def post(s, x1):
        f = jnp.dot(act_ref[s, :], wfo_ref[...], preferred_element_type=_F32)
        o_ref[0, s, :] = _layer_norm(ALPHA * x1 + gt2 * f, g2_ref[...], b2_ref[...], LN_EPS)

    nxt = pre(slabs[0])
    for i, s in enumerate(slabs):
        x1, h = nxt
        if i + 1 < FFN_SLABS:
            nxt = pre(slabs[i + 1])
        hidden(s, h)
        post(s, x1)


def _ffn_call(x, y, mod3, eg, eb, wout, g1, b1, wfi, wfo, g2, b2):
    B, S, D = x.shape
    tm = FFN_TILE
    const2 = lambda b, t: (0, 0)
    tile = lambda b, t: (b, t, 0)

    def resident(a):
        return pl.BlockSpec(a.shape, const2, pipeline_mode=pl.Buffered(1))

    return pl.pallas_call(
        _ffn_kernel,
        grid=(B, S // tm),
        in_specs=[pl.BlockSpec((1, tm, D), tile),
                  pl.BlockSpec((1, tm, D), tile),
                  pl.BlockSpec((1, 6, D), lambda b, t: (b, 0, 0)),
                  resident(eg), resident(eb), resident(wout), resident(g1), resident(b1),
                  resident(wfi), resident(wfo), resident(g2), resident(b2)],
        out_specs=pl.BlockSpec((1, tm, D), tile),
        out_shape=jax.ShapeDtypeStruct((B, S, D), _F32),
        scratch_shapes=[pltpu.VMEM((tm, FFN_HIDDEN), _BF16)],
        compiler_params=pltpu.CompilerParams(
            dimension_semantics=("arbitrary", "arbitrary"),
            vmem_limit_bytes=VMEM_LIMIT_BYTES),
        name="proj_ffn",
    )(x, y, mod3, eg, eb, wout, g1, b1, wfi, wfo, g2, b2)


def kernel(x, c, emb_ln_g, emb_ln_b, w_ada, b_ada, w_in, mu_shift, sg_ln_g, sg_ln_b, w_s, b_s,
           w0, w_up, a0, a_up, g_up, k_k, k_a, r_k, lnx_g, lnx_b, w_out, ln1_g, ln1_b,
           w_ffn_in, w_ffn_out, ln2_g, ln2_b):
    B, S, D = x.shape
    row = lambda a: a.reshape(1, -1).astype(_F32)
    l = 0
    pad_cols = RWKV_COLS_PAD - RWKV_COLS

    win_p = jnp.pad(w_in[l], ((0, 0), (0, pad_cols))).astype(_BF16)
    mu_p = jnp.pad(mu_shift[l], (0, pad_cols)).reshape(1, -1)
    wwa = jnp.zeros((LORA_W + LORA_A, 2 * MIX_B), _F32)
    wwa = wwa.at[:LORA_W, :MIX_B].set(w_up[l]).at[LORA_W:, MIX_B:].set(a_up[l]).astype(_BF16)
    gup_p = jnp.pad(g_up[l], ((0, LORA_G_PAD - LORA_G), (0, 0))).astype(_BF16)
    bs_b = jnp.broadcast_to(b_s[l][:, :, None], (A_GROUPS, GMLP_CHUNK, GMLP_CHUNK))

    mod3 = _ada_call(c, w_ada[l], b_ada[l]).reshape(B, 6, D)
    y = _mix_call(x, mod3, row(emb_ln_g), row(emb_ln_b), win_p, mu_p, row(sg_ln_g[l]),
                  row(sg_ln_b[l]), w_s[l], bs_b, row(w0[l]), wwa, row(a0[l]), gup_p,
                  row(k_k[l]), row(k_a[l]), row(r_k[l]), row(lnx_g[l]), row(lnx_b[l]))
    return _ffn_call(x, y, mod3, row(emb_ln_g), row(emb_ln_b), w_out[l].astype(_BF16),
                     row(ln1_g[l]), row(ln1_b[l]), w_ffn_in[l].astype(_BF16),
                     w_ffn_out[l].astype(_BF16), row(ln2_g[l]), row(ln2_b[l]))
```

```python
import math

import jax
import jax.numpy as jnp
from jax import lax
from jax.experimental import pallas as pl
from jax.experimental.pallas import tpu as pltpu

D_MODEL = 1024
MIX_A = 512
MIX_B = 512
GMLP_CHUNK = 128
A_GROUPS = 4
HEAD_DIM = 64
HEADS = MIX_B // HEAD_DIM
LORA_W = 64
LORA_A = 64
LORA_G = 160
LORA_G_PAD = 256
RWKV_COLS = 3 * MIX_B + LORA_W + LORA_A + LORA_G
RWKV_COLS_PAD = 3 * MIX_B + LORA_W + LORA_A + LORA_G_PAD
IN_COLS_PAD = 2 * MIX_A + RWKV_COLS_PAD
FFN_HIDDEN = 2816
FFN_CHUNK = 256
DEPTH = 1
ALPHA = (2 * DEPTH) ** 0.25
LN_EPS = 1e-5
GN_EPS = 64e-5
DECAY_SCALE = math.exp(-0.5)

RCHUNK = 64
PAIR = 2 * HEAD_DIM
MIX_BATCH = 4
SEQ_TILE = 256
FFN_TILE = 512
FFN_SLABS = 2
VMEM_LIMIT_BYTES = 56 * 1024 * 1024

_F32 = jnp.float32
_BF16 = jnp.bfloat16


def _layer_norm(x, g, b, eps):
    mu = jnp.mean(x, axis=-1, keepdims=True)
    d = x - mu
    var = jnp.mean(d * d, axis=-1, keepdims=True)
    return d * lax.rsqrt(var + eps) * g + b


def _gelu_tanh(x):
    c = math.sqrt(2.0 / math.pi)
    half = 0.5 * x
    return half + half * jnp.tanh(x * (c + (c * 0.044715) * (x * x)))


def _dot(a, b):
    return jnp.dot(a.astype(_BF16), b.astype(_BF16), preferred_element_type=_F32)


def _dot_nt(a, b):
    return lax.dot_general(a.astype(_BF16), b.astype(_BF16), (((1,), (1,)), ((), ())),
                           preferred_element_type=_F32)


def _dot_tn(a, b):
    return lax.dot_general(a.astype(_BF16), b.astype(_BF16), (((0,), (0,)), ((), ())),
                           preferred_element_type=_F32)


def _ada_kernel(c_ref, w_ref, b_ref, o_ref):
    c = c_ref[...]
    cs = c * jax.nn.sigmoid(c)
    o_ref[...] = _dot(cs, w_ref[...]) + b_ref[...]


def _ada_call(c, w_ada, b_ada):
    B, D = c.shape
    n = w_ada.shape[1]
    tn = 1024
    return pl.pallas_call(
        _ada_kernel,
        grid=(n // tn,),
        in_specs=[pl.BlockSpec((B, D), lambda j: (0, 0)),
                  pl.BlockSpec((D, tn), lambda j: (0, j)),
                  pl.BlockSpec((1, tn), lambda j: (0, j))],
        out_specs=pl.BlockSpec((B, tn), lambda j: (0, j)),
        out_shape=jax.ShapeDtypeStruct((B, n), _F32),
        name="ada_mod",
    )(c, w_ada, b_ada.reshape(1, n))


def _mix_kernel(x_ref, mod_ref, eg_ref, eb_ref, win_ref, mu_ref, sgg_ref, sgb_ref, ws_ref, bs_ref,
                w0_ref, wwa_ref, a0_ref, gup_ref, kk_ref, ka_ref, rk_ref, lg_ref, lb_ref,
                y_ref, carry_ref, state_ref):
    nb, ts = x_ref.shape[0], x_ref.shape[1]
    rows = nb * RCHUNK
    n_pairs = HEADS // 2

    @pl.when(pl.program_id(1) == 0)
    def _():
        carry_ref[...] = jnp.zeros_like(carry_ref)
        state_ref[...] = jnp.zeros_like(state_ref)

    hr = jnp.right_shift(lax.broadcasted_iota(jnp.int32, (2 * PAIR, 2 * PAIR), 0), 6)
    hc = jnp.right_shift(lax.broadcasted_iota(jnp.int32, (2 * PAIR, 2 * PAIR), 1), 6)
    head_ones = jnp.where(hr == hc, 1.0, 0.0).astype(_BF16)

    def head_sums(*arrays):
        halves = [a[:, i:i + 2 * PAIR] for a in arrays for i in range(0, MIX_B, 2 * PAIR)]
        out = _dot(jnp.concatenate(halves, axis=0), head_ones)
        m = arrays[0].shape[0]
        parts = [out[j * m:(j + 1) * m] for j in range(len(halves))]
        per = MIX_B // (2 * PAIR)
        return [jnp.concatenate(parts[j * per:(j + 1) * per], axis=1)
                for j in range(len(arrays))]

    def head_sum(a):
        return head_sums(a)[0]

    row = lax.broadcasted_iota(jnp.int32, (GMLP_CHUNK, GMLP_CHUNK), 0)
    col = lax.broadcasted_iota(jnp.int32, (GMLP_CHUNK, GMLP_CHUNK), 1)
    wm = [jnp.where(row >= col, ws_ref[g], 0.0).astype(_BF16) for g in range(A_GROUPS)]

    ri = lax.broadcasted_iota(jnp.int32, (rows, rows), 0)
    ci = lax.broadcasted_iota(jnp.int32, (rows, rows), 1)
    lmat = jnp.where((jnp.right_shift(ri, 6) == jnp.right_shift(ci, 6)) & (ri >= ci),
                     1.0, 0.0).astype(_BF16)

    gi = lax.broadcasted_iota(jnp.int32, (PAIR, PAIR), 0)
    gj = lax.broadcasted_iota(jnp.int32, (PAIR, PAIR), 1)
    gim = jnp.bitwise_and(gi, RCHUNK - 1)
    gjm = jnp.bitwise_and(gj, RCHUNK - 1)
    gmask = (gim > gjm) | ((gi >= RCHUNK) & (gim == gjm))
    bdmask = (gi < RCHUNK) == (gj < RCHUNK)
    left = gj < RCHUNK
    left_c = lax.broadcasted_iota(jnp.int32, (RCHUNK, PAIR), 1) < RCHUNK

    def lsel(a):
        return jnp.where(left if a.shape[0] == PAIR else left_c, a, jnp.zeros_like(a))

    def rsel(a):
        return jnp.where(left if a.shape[0] == PAIR else left_c, jnp.zeros_like(a), a)

    def prepare(c, vg_prev):
        tok = slice(c * RCHUNK, (c + 1) * RCHUNK)
        h_parts = []
        for b in range(nb):
            x0 = _layer_norm(x_ref[b, tok, :], eg_ref[...], eb_ref[...], LN_EPS)
            h_parts.append((x0 * (1.0 + mod_ref[b, 1:2, :]) + mod_ref[b, 0:1, :]).astype(_BF16))
        h = jnp.concatenate(h_parts, axis=0)
        p = jnp.dot(h, win_ref[...], preferred_element_type=_F32)

        u = _gelu_tanh(p[:, 0:MIX_A])
        vg = _layer_norm(_gelu_tanh(p[:, MIX_A:2 * MIX_A]), sgg_ref[...], sgb_ref[...],
                         LN_EPS).astype(_BF16)
        second = c % 2 == 1
        hrow = slice(RCHUNK, 2 * RCHUNK) if second else slice(0, RCHUNK)
        bslice = [slice(b * RCHUNK, (b + 1) * RCHUNK) for b in range(nb)]
        for g in range(A_GROUPS):
            cs = slice(g * GMLP_CHUNK, (g + 1) * GMLP_CHUNK)
            if second:
                vcat = jnp.concatenate(
                    [jnp.concatenate([vg_prev[rs, cs], vg[rs, cs]], axis=0) for rs in bslice],
                    axis=1)
                s_all = jnp.dot(wm[g][hrow, :], vcat, preferred_element_type=_F32)
            else:
                vcat = jnp.concatenate([vg[rs, cs] for rs in bslice], axis=1)
                s_all = jnp.dot(wm[g][hrow, 0:RCHUNK], vcat, preferred_element_type=_F32)
            for b, rs in enumerate(bslice):
                s = s_all[:, b * GMLP_CHUNK:(b + 1) * GMLP_CHUNK] + bs_ref[g, hrow, :]
                y_ref[b, tok, cs] = (u[rs, cs] * s).astype(y_ref.dtype)

        ps = p[:, 2 * MIX_A:]
        prev = pltpu.roll(ps, 1, 0)
        rid = lax.broadcasted_iota(jnp.int32, ps.shape, 0)
        for b in range(nb):
            prev = jnp.where(rid == b * RCHUNK, carry_ref[8 * b:8 * b + 1, :], prev)
        for b in range(nb):
            carry_ref[8 * b:8 * b + 1, :] = ps[(b + 1) * RCHUNK - 1:(b + 1) * RCHUNK, :]
        xs = ps + (prev - ps) * mu_ref[...]

        r = xs[:, 0:MIX_B]
        k = xs[:, MIX_B:2 * MIX_B]
        v = xs[:, 2 * MIX_B:3 * MIX_B]
        wa = xs[:, 3 * MIX_B:3 * MIX_B + 128]
        gd = xs[:, 3 * MIX_B + 128:]

        lane = lax.broadcasted_iota(jnp.int32, wa.shape, 1)
        wa_act = jnp.where(lane < LORA_W, jnp.tanh(wa), wa)
        up = _dot(wa_act, wwa_ref[...])
        sig_w = jax.nn.sigmoid(w0_ref[...] + up[:, 0:MIX_B])
        lr = jax.nn.sigmoid(a0_ref[...] + up[:, MIX_B:])
        gate = _dot(jax.nn.sigmoid(gd), gup_ref[...])

        kk = k * kk_ref[...]
        k2 = k * (1.0 + (lr - 1.0) * ka_ref[...])
        kk_ss, rk_sum = head_sums(kk * kk, r * k2 * rk_ref[...])
        kk = kk * lax.rsqrt(jnp.maximum(kk_ss, 1e-24))
        bonus = rk_sum * v

        sw_hi = sig_w.astype(_BF16)
        sw_lo = (sig_w - sw_hi.astype(_F32)).astype(_BF16)
        csum2 = jnp.dot(lmat, jnp.concatenate([sw_hi, sw_lo], axis=1),
                        preferred_element_type=_F32)
        csum = csum2[:, 0:MIX_B] + csum2[:, MIX_B:]
        cum = -DECAY_SCALE * csum
        e_pos = jnp.exp(cum)
        e_neg = jnp.exp(-cum)
        e_exc = jnp.exp(cum + DECAY_SCALE * sig_w)
        ops = dict(at=(-kk * e_exc).astype(_BF16), rt=(r * e_pos).astype(_BF16),
                   bt=(kk * lr * e_neg).astype(_BF16), kt=(k2 * e_neg).astype(_BF16),
                   v=v.astype(_BF16), e_pos=e_pos, bonus=bonus, gate=gate)
        return ops, vg

    def recur(ops, state):
        lp, r_e, r_o, vp, wend = [], [], [], [], []
        for b in range(nb):
            rs = slice(b * RCHUNK, (b + 1) * RCHUNK)
            last = (b + 1) * RCHUNK - 1
            for q in range(n_pairs):
                ls = slice(q * PAIR, (q + 1) * PAIR)
                lp.append(jnp.concatenate([ops["at"][rs, ls], ops["rt"][rs, ls]], axis=0))
                r_e.append(jnp.concatenate([ops["kt"][rs, ls], ops["bt"][rs, ls]], axis=0))
                r_o.append(jnp.concatenate([ops["bt"][rs, ls], ops["kt"][rs, ls]], axis=0))
                vp.append(ops["v"][rs, ls])
                wend.append(ops["e_pos"][last:last + 1, ls])
        n = range(len(lp))
        bk = [jnp.concatenate([lsel(r_e[i]), rsel(r_o[i])], axis=0) for i in n]
        gah = [_dot_nt(lp[i], jnp.concatenate([bk[i], state[i].astype(_BF16)], axis=0))
               for i in n]
        ah = [gah[i][:, 2 * PAIR:] for i in n]
        gm_e = [jnp.where(gmask, gah[i][:, 0:PAIR], 0.0).astype(_BF16) for i in n]
        gm_o = [jnp.where(gmask, gah[i][:, PAIR:2 * PAIR], 0.0).astype(_BF16) for i in n]
        vbd = [jnp.concatenate([lsel(vp[i]), rsel(vp[i])], axis=0) for i in n]
        akv = [_dot(jnp.where(left_c, gm_e[i][0:RCHUNK], gm_o[i][0:RCHUNK]), vbd[i]) for i in n]
        x0 = [ah[i][0:RCHUNK] + akv[i] for i in n]
        z = [jnp.concatenate([jnp.where(left_c, gm_o[i][0:RCHUNK].astype(_F32), x0[i]),
                              jnp.where(left_c, x0[i], gm_e[i][0:RCHUNK].astype(_F32))], axis=0)
             for i in n]
        for _ in range(6):
            zb = [z[i].astype(_BF16) for i in n]
            prod = [_dot(jnp.where(bdmask, zb[i], jnp.zeros_like(zb[i])), zb[i]) for i in n]
            z = [prod[i] + jnp.where(bdmask, 0.0, z[i]) for i in n]
        uv = [jnp.concatenate([lsel(vp[i]), lsel(z[i][RCHUNK:].astype(_BF16)),
                               rsel(z[i][0:RCHUNK].astype(_BF16)), rsel(vp[i])], axis=0)
              for i in n]
        y_pairs = [ah[i][RCHUNK:]
                   + _dot(jnp.concatenate([gm_e[i][RCHUNK:], gm_o[i][RCHUNK:]], axis=1), uv[i])
                   for i in n]
        new_state = [(state[i] + _dot_tn(uv[i], bk[i])) * wend[i] for i in n]
        yv = jnp.concatenate([jnp.concatenate(y_pairs[b * n_pairs:(b + 1) * n_pairs], axis=1)
                              for b in range(nb)], axis=0)
        return yv, new_state

    def finish(c, yv, ops):
        inv = 1.0 / HEAD_DIM
        mean = head_sum(yv) * inv
        d = yv - mean
        var = head_sum(d * d) * inv
        yn = d * lax.rsqrt(var + GN_EPS) * lg_ref[...] + lb_ref[...]
        yb = ((yn + ops["bonus"]) * ops["gate"]).astype(y_ref.dtype)
        for b in range(nb):
            y_ref[b, c * RCHUNK:(c + 1) * RCHUNK, MIX_A:] = yb[b * RCHUNK:(b + 1) * RCHUNK]

    state = [state_ref[i] for i in range(nb * n_pairs)]
    vg_prev = None
    for c in range(ts // RCHUNK):
        ops, vg_prev = prepare(c, vg_prev)
        yv, state = recur(ops, state)
        finish(c, yv, ops)
    for i in range(nb * n_pairs):
        state_ref[i] = state[i]


def _mix_call(x, mod3, eg, eb, win_p, mu_p, sgg, sgb, ws, bs_b, w0, wwa, a0, gup_p, kkp, kap, rkp,
              lg, lb):
    B, S, D = x.shape
    nb = math.gcd(B, MIX_BATCH)
    ts = SEQ_TILE
    assert ts % GMLP_CHUNK == 0 and S % ts == 0
    const2 = lambda b, t: (0, 0)
    const3 = lambda b, t: (0, 0, 0)

    def full(a):
        return pl.BlockSpec(a.shape, const2 if a.ndim == 2 else const3)

    in_specs = [pl.BlockSpec((nb, ts, D), lambda b, t: (b, t, 0)),
                pl.BlockSpec((nb, 6, D), lambda b, t: (b, 0, 0))]
    consts = [eg, eb, win_p, mu_p, sgg, sgb, ws, bs_b, w0, wwa, a0, gup_p, kkp, kap, rkp, lg, lb]
    in_specs += [full(a) for a in consts]
    scratch = [pltpu.VMEM((8 * nb, RWKV_COLS_PAD), _F32),
               pltpu.VMEM((nb * HEADS // 2, PAIR, PAIR), _F32)]
    return pl.pallas_call(
        _mix_kernel,
        grid=(B // nb, S // ts),
        in_specs=in_specs,
        out_specs=pl.BlockSpec((nb, ts, D), lambda b, t: (b, t, 0)),
        out_shape=jax.ShapeDtypeStruct((B, S, D), _BF16),
        scratch_shapes=scratch,
        compiler_params=pltpu.CompilerParams(
            dimension_semantics=("arbitrary", "arbitrary"),
            vmem_limit_bytes=VMEM_LIMIT_BYTES),
        name="token_mix",
    )(x, mod3, *consts)


def _ffn_kernel(x_ref, y_ref, mod_ref, eg_ref, eb_ref, wout_ref, g1_ref, b1_ref, wfi_ref, wfo_ref,
                g2_ref, b2_ref, o_ref, act_ref):
    gt1 = mod_ref[0, 2:3, :]
    sh2 = mod_ref[0, 3:4, :]
    sc2 = mod_ref[0, 4:5, :]
    gt2 = mod_ref[0, 5:6, :]
    tm = x_ref.shape[1]
    slabs = [slice(i, i + tm // FFN_SLABS) for i in range(0, tm, tm // FFN_SLABS)]

    def pre(s):
        x0 = _layer_norm(x_ref[0, s, :], eg_ref[...], eb_ref[...], LN_EPS)
        mix = jnp.dot(y_ref[0, s, :], wout_ref[...], preferred_element_type=_F32)
        x1 = _layer_norm(ALPHA * x0 + gt1 * mix, g1_ref[...], b1_ref[...], LN_EPS)
        return x1, (x1 * (1.0 + sc2) + sh2).astype(_BF16)

    def hidden(s, h):
        for c in range(FFN_HIDDEN // FFN_CHUNK):
            gcol = pl.ds(c * FFN_CHUNK, FFN_CHUNK)
            ucol = pl.ds(FFN_HIDDEN + c * FFN_CHUNK, FFN_CHUNK)
            gg = jnp.dot(h, wfi_ref[:, gcol], preferred_element_type=_F32)
            uu = jnp.dot(h, wfi_ref[:, ucol], preferred_element_type=_F32)
            act_ref[s, gcol] = (gg * jax.nn.sigmoid(gg) * uu).astype(_BF16)

    def post(s, x1):
        f = jnp.dot(act_ref[s, :], wfo_ref[...], preferred_element_type=_F32)
        o_ref[0, s, :] = _layer_norm(ALPHA * x1 + gt2 * f, g2_ref[...], b2_ref[...], LN_EPS)

    nxt = pre(slabs[0])
    for i, s in enumerate(slabs):
        x1, h = nxt
        if i + 1 < FFN_SLABS:
            nxt = pre(slabs[i + 1])
        hidden(s, h)
        post(s, x1)


def _ffn_call(x, y, mod3, eg, eb, wout, g1, b1, wfi, wfo, g2, b2):
    B, S, D = x.shape
    tm = FFN_TILE
    const2 = lambda b, t: (0, 0)
    tile = lambda b, t: (b, t, 0)

    def resident(a):
        return pl.BlockSpec(a.shape, const2, pipeline_mode=pl.Buffered(1))

    return pl.pallas_call(
        _ffn_kernel,
        grid=(B, S // tm),
        in_specs=[pl.BlockSpec((1, tm, D), tile),
                  pl.BlockSpec((1, tm, D), tile),
                  pl.BlockSpec((1, 6, D), lambda b, t: (b, 0, 0)),
                  resident(eg), resident(eb), resident(wout), resident(g1), resident(b1),
                  resident(wfi), resident(wfo), resident(g2), resident(b2)],
        out_specs=pl.BlockSpec((1, tm, D), tile),
        out_shape=jax.ShapeDtypeStruct((B, S, D), _F32),
        scratch_shapes=[pltpu.VMEM((tm, FFN_HIDDEN), _BF16)],
        compiler_params=pltpu.CompilerParams(
            dimension_semantics=("arbitrary", "arbitrary"),
            vmem_limit_bytes=VMEM_LIMIT_BYTES),
        name="proj_ffn",
    )(x, y, mod3, eg, eb, wout, g1, b1, wfi, wfo, g2, b2)


def kernel(x, c, emb_ln_g, emb_ln_b, w_ada, b_ada, w_in, mu_shift, sg_ln_g, sg_ln_b, w_s, b_s,
           w0, w_up, a0, a_up, g_up, k_k, k_a, r_k, lnx_g, lnx_b, w_out, ln1_g, ln1_b,
           w_ffn_in, w_ffn_out, ln2_g, ln2_b):
    B, S, D = x.shape
    assert w_in.shape[0] == DEPTH == 1
    row = lambda a: a.reshape(1, -1).astype(_F32)
    layer = lambda a: a.reshape(a.shape[1:])
    pad_cols = RWKV_COLS_PAD - RWKV_COLS

    win_p = jnp.pad(layer(w_in).astype(_BF16), ((0, 0), (0, pad_cols)))
    mu_p = jnp.pad(layer(mu_shift), (0, pad_cols)).reshape(1, -1)
    wwa = jnp.zeros((LORA_W + LORA_A, 2 * MIX_B), _F32)
    wwa = wwa.at[:LORA_W, :MIX_B].set(layer(w_up)).at[LORA_W:, MIX_B:].set(layer(a_up))
    wwa = wwa.astype(_BF16)
    gup_p = jnp.pad(layer(g_up), ((0, LORA_G_PAD - LORA_G), (0, 0))).astype(_BF16)
    bs_b = jnp.broadcast_to(layer(b_s)[:, :, None], (A_GROUPS, GMLP_CHUNK, GMLP_CHUNK))

    mod3 = _ada_call(c, layer(w_ada), layer(b_ada)).reshape(B, 6, D)
    y = _mix_call(x, mod3, row(emb_ln_g), row(emb_ln_b), win_p, mu_p, row(sg_ln_g),
                  row(sg_ln_b), layer(w_s), bs_b, row(w0), wwa, row(a0), gup_p,
                  row(k_k), row(k_a), row(r_k), row(lnx_g), row(lnx_b))
    return _ffn_call(x, y, mod3, row(emb_ln_g), row(emb_ln_b), layer(w_out).astype(_BF16),
                     row(ln1_g), row(ln1_b), layer(w_ffn_in).astype(_BF16),
                     layer(w_ffn_out).astype(_BF16), row(ln2_g), row(ln2_b))
```

```python
import math

import jax
import jax.numpy as jnp
from jax import lax
from jax.experimental import pallas as pl
from jax.experimental.pallas import tpu as pltpu

D_MODEL = 1024
MIX_A = 512
MIX_B = 512
GMLP_CHUNK = 128
A_GROUPS = 4
HEAD_DIM = 64
HEADS = MIX_B // HEAD_DIM
LORA_W = 64
LORA_A = 64
LORA_G = 160
LORA_G_PAD = 256
RWKV_COLS = 3 * MIX_B + LORA_W + LORA_A + LORA_G
RWKV_COLS_PAD = 3 * MIX_B + LORA_W + LORA_A + LORA_G_PAD
IN_COLS_PAD = 2 * MIX_A + RWKV_COLS_PAD
FFN_HIDDEN = 2816
FFN_CHUNK = 256
DEPTH = 1
ALPHA = (2 * DEPTH) ** 0.25
LN_EPS = 1e-5
GN_EPS = 64e-5
DECAY_SCALE = math.exp(-0.5)

RCHUNK = 64
PAIR = 2 * HEAD_DIM
MIX_BATCH = 4
SEQ_TILE = 256
FFN_TILE = 1024
FFN_SLABS = 4
VMEM_LIMIT_BYTES = 56 * 1024 * 1024

_F32 = jnp.float32
_BF16 = jnp.bfloat16


def _layer_norm(x, g, b, eps):
    mu = jnp.mean(x, axis=-1, keepdims=True)
    d = x - mu
    var = jnp.mean(d * d, axis=-1, keepdims=True)
    return d * lax.rsqrt(var + eps) * g + b


def _gelu_tanh(x):
    c = math.sqrt(2.0 / math.pi)
    half = 0.5 * x
    return half + half * jnp.tanh(x * (c + (c * 0.044715) * (x * x)))


def _dot(a, b):
    return jnp.dot(a.astype(_BF16), b.astype(_BF16), preferred_element_type=_F32)


def _dot_nt(a, b):
    return lax.dot_general(a.astype(_BF16), b.astype(_BF16), (((1,), (1,)), ((), ())),
                           preferred_element_type=_F32)


def _dot_tn(a, b):
    return lax.dot_general(a.astype(_BF16), b.astype(_BF16), (((0,), (0,)), ((), ())),
                           preferred_element_type=_F32)


def _ada_kernel(c_ref, w_ref, b_ref, o_ref):
    c = c_ref[...]
    cs = c * jax.nn.sigmoid(c)
    o_ref[...] = _dot(cs, w_ref[...]) + b_ref[...]


def _ada_call(c, w_ada, b_ada):
    B, D = c.shape
    n = w_ada.shape[1]
    tn = 1024
    return pl.pallas_call(
        _ada_kernel,
        grid=(n // tn,),
        in_specs=[pl.BlockSpec((B, D), lambda j: (0, 0)),
                  pl.BlockSpec((D, tn), lambda j: (0, j)),
                  pl.BlockSpec((1, tn), lambda j: (0, j))],
        out_specs=pl.BlockSpec((B, tn), lambda j: (0, j)),
        out_shape=jax.ShapeDtypeStruct((B, n), _F32),
        name="ada_mod",
    )(c, w_ada, b_ada.reshape(1, n))


def _mix_kernel(x_ref, mod_ref, eg_ref, eb_ref, win_ref, mu_ref, sgg_ref, sgb_ref, ws_ref, bs_ref,
                w0_ref, wwa_ref, a0_ref, gup_ref, kk_ref, ka_ref, rk_ref, lg_ref, lb_ref,
                y_ref, carry_ref, state_ref):
    nb, ts = x_ref.shape[0], x_ref.shape[1]
    rows = nb * RCHUNK
    n_pairs = HEADS // 2

    @pl.when(pl.program_id(1) == 0)
    def _():
        carry_ref[...] = jnp.zeros_like(carry_ref)
        state_ref[...] = jnp.zeros_like(state_ref)

    hr = jnp.right_shift(lax.broadcasted_iota(jnp.int32, (2 * PAIR, 2 * PAIR), 0), 6)
    hc = jnp.right_shift(lax.broadcasted_iota(jnp.int32, (2 * PAIR, 2 * PAIR), 1), 6)
    head_ones = jnp.where(hr == hc, 1.0, 0.0).astype(_BF16)

    def head_sums(*arrays):
        halves = [a[:, i:i + 2 * PAIR] for a in arrays for i in range(0, MIX_B, 2 * PAIR)]
        out = _dot(jnp.concatenate(halves, axis=0), head_ones)
        m = arrays[0].shape[0]
        parts = [out[j * m:(j + 1) * m] for j in range(len(halves))]
        per = MIX_B // (2 * PAIR)
        return [jnp.concatenate(parts[j * per:(j + 1) * per], axis=1)
                for j in range(len(arrays))]

    def head_sum(a):
        return head_sums(a)[0]

    row = lax.broadcasted_iota(jnp.int32, (GMLP_CHUNK, GMLP_CHUNK), 0)
    col = lax.broadcasted_iota(jnp.int32, (GMLP_CHUNK, GMLP_CHUNK), 1)
    wm = [jnp.where(row >= col, ws_ref[g], 0.0).astype(_BF16) for g in range(A_GROUPS)]

    ri = lax.broadcasted_iota(jnp.int32, (rows, rows), 0)
    ci = lax.broadcasted_iota(jnp.int32, (rows, rows), 1)
    lmat = jnp.where((jnp.right_shift(ri, 6) == jnp.right_shift(ci, 6)) & (ri >= ci),
                     1.0, 0.0).astype(_BF16)

    gi = lax.broadcasted_iota(jnp.int32, (PAIR, PAIR), 0)
    gj = lax.broadcasted_iota(jnp.int32, (PAIR, PAIR), 1)
    gim = jnp.bitwise_and(gi, RCHUNK - 1)
    gjm = jnp.bitwise_and(gj, RCHUNK - 1)
    gmask = (gim > gjm) | ((gi >= RCHUNK) & (gim == gjm))
    bdmask = (gi < RCHUNK) == (gj < RCHUNK)
    left = gj < RCHUNK
    left_c = lax.broadcasted_iota(jnp.int32, (RCHUNK, PAIR), 1) < RCHUNK

    def lsel(a):
        return jnp.where(left if a.shape[0] == PAIR else left_c, a, jnp.zeros_like(a))

    def rsel(a):
        return jnp.where(left if a.shape[0] == PAIR else left_c, jnp.zeros_like(a), a)

    def prepare(c, vg_prev):
        tok = slice(c * RCHUNK, (c + 1) * RCHUNK)
        h_parts = []
        for b in range(nb):
            x0 = _layer_norm(x_ref[b, tok, :], eg_ref[...], eb_ref[...], LN_EPS)
            h_parts.append((x0 * (1.0 + mod_ref[b, 1:2, :]) + mod_ref[b, 0:1, :]).astype(_BF16))
        h = jnp.concatenate(h_parts, axis=0)
        p = jnp.dot(h, win_ref[...], preferred_element_type=_F32)

        u = _gelu_tanh(p[:, 0:MIX_A])
        vg = _layer_norm(_gelu_tanh(p[:, MIX_A:2 * MIX_A]), sgg_ref[...], sgb_ref[...],
                         LN_EPS).astype(_BF16)
        second = c % 2 == 1
        hrow = slice(RCHUNK, 2 * RCHUNK) if second else slice(0, RCHUNK)
        bslice = [slice(b * RCHUNK, (b + 1) * RCHUNK) for b in range(nb)]
        for g in range(A_GROUPS):
            cs = slice(g * GMLP_CHUNK, (g + 1) * GMLP_CHUNK)
            if second:
                vcat = jnp.concatenate(
                    [jnp.concatenate([vg_prev[rs, cs], vg[rs, cs]], axis=0) for rs in bslice],
                    axis=1)
                s_all = jnp.dot(wm[g][hrow, :], vcat, preferred_element_type=_F32)
            else:
                vcat = jnp.concatenate([vg[rs, cs] for rs in bslice], axis=1)
                s_all = jnp.dot(wm[g][hrow, 0:RCHUNK], vcat, preferred_element_type=_F32)
            for b, rs in enumerate(bslice):
                s = s_all[:, b * GMLP_CHUNK:(b + 1) * GMLP_CHUNK] + bs_ref[g, hrow, :]
                y_ref[b, tok, cs] = (u[rs, cs] * s).astype(y_ref.dtype)

        ps = p[:, 2 * MIX_A:]
        prev = pltpu.roll(ps, 1, 0)
        rid = lax.broadcasted_iota(jnp.int32, ps.shape, 0)
        for b in range(nb):
            prev = jnp.where(rid == b * RCHUNK, carry_ref[8 * b:8 * b + 1, :], prev)
        for b in range(nb):
            carry_ref[8 * b:8 * b + 1, :] = ps[(b + 1) * RCHUNK - 1:(b + 1) * RCHUNK, :]
        xs = ps + (prev - ps) * mu_ref[...]

        r = xs[:, 0:MIX_B]
        k = xs[:, MIX_B:2 * MIX_B]
        v = xs[:, 2 * MIX_B:3 * MIX_B]
        wa = xs[:, 3 * MIX_B:3 * MIX_B + 128]
        gd = xs[:, 3 * MIX_B + 128:]

        lane = lax.broadcasted_iota(jnp.int32, wa.shape, 1)
        wa_act = jnp.where(lane < LORA_W, jnp.tanh(wa), wa)
        up = _dot(wa_act, wwa_ref[...])
        sig_w = jax.nn.sigmoid(w0_ref[...] + up[:, 0:MIX_B])
        lr = jax.nn.sigmoid(a0_ref[...] + up[:, MIX_B:])
        gate = _dot(jax.nn.sigmoid(gd), gup_ref[...])

        kk = k * kk_ref[...]
        k2 = k * (1.0 + (lr - 1.0) * ka_ref[...])
        kk_ss, rk_sum = head_sums(kk * kk, r * k2 * rk_ref[...])
        kk = kk * lax.rsqrt(jnp.maximum(kk_ss, 1e-24))
        bonus = rk_sum * v

        sw_hi = sig_w.astype(_BF16)
        sw_lo = (sig_w - sw_hi.astype(_F32)).astype(_BF16)
        csum2 = jnp.dot(lmat, jnp.concatenate([sw_hi, sw_lo], axis=1),
                        preferred_element_type=_F32)
        csum = csum2[:, 0:MIX_B] + csum2[:, MIX_B:]
        cum = -DECAY_SCALE * csum
        e_pos = jnp.exp(cum)
        e_neg = jnp.exp(-cum)
        e_exc = jnp.exp(cum + DECAY_SCALE * sig_w)
        ops = dict(at=(-kk * e_exc).astype(_BF16), rt=(r * e_pos).astype(_BF16),
                   bt=(kk * lr * e_neg).astype(_BF16), kt=(k2 * e_neg).astype(_BF16),
                   v=v.astype(_BF16), e_pos=e_pos, bonus=bonus, gate=gate)
        return ops, vg

    def recur(ops, state):
        lp, r_e, r_o, vp, wend = [], [], [], [], []
        for b in range(nb):
            rs = slice(b * RCHUNK, (b + 1) * RCHUNK)
            last = (b + 1) * RCHUNK - 1
            for q in range(n_pairs):
                ls = slice(q * PAIR, (q + 1) * PAIR)
                lp.append(jnp.concatenate([ops["at"][rs, ls], ops["rt"][rs, ls]], axis=0))
                r_e.append(jnp.concatenate([ops["kt"][rs, ls], ops["bt"][rs, ls]], axis=0))
                r_o.append(jnp.concatenate([ops["bt"][rs, ls], ops["kt"][rs, ls]], axis=0))
                vp.append(ops["v"][rs, ls])
                wend.append(ops["e_pos"][last:last + 1, ls])
        n = range(len(lp))
        bk = [jnp.concatenate([lsel(r_e[i]), rsel(r_o[i])], axis=0) for i in n]
        gah = [_dot_nt(lp[i], jnp.concatenate([bk[i], state[i].astype(_BF16)], axis=0))
               for i in n]
        ah = [gah[i][:, 2 * PAIR:] for i in n]
        gm_e = [jnp.where(gmask, gah[i][:, 0:PAIR], 0.0).astype(_BF16) for i in n]
        gm_o = [jnp.where(gmask, gah[i][:, PAIR:2 * PAIR], 0.0).astype(_BF16) for i in n]
        vbd = [jnp.concatenate([lsel(vp[i]), rsel(vp[i])], axis=0) for i in n]
        akv = [_dot(jnp.where(left_c, gm_e[i][0:RCHUNK], gm_o[i][0:RCHUNK]), vbd[i]) for i in n]
        x0 = [ah[i][0:RCHUNK] + akv[i] for i in n]
        z = [jnp.concatenate([jnp.where(left_c, gm_o[i][0:RCHUNK].astype(_F32), x0[i]),
                              jnp.where(left_c, x0[i], gm_e[i][0:RCHUNK].astype(_F32))], axis=0)
             for i in n]
        for _ in range(6):
            zb = [z[i].astype(_BF16) for i in n]
            prod = [_dot(jnp.where(bdmask, zb[i], jnp.zeros_like(zb[i])), zb[i]) for i in n]
            z = [prod[i] + jnp.where(bdmask, 0.0, z[i]) for i in n]
        uv = [jnp.concatenate([lsel(vp[i]), lsel(z[i][RCHUNK:].astype(_BF16)),
                               rsel(z[i][0:RCHUNK].astype(_BF16)), rsel(vp[i])], axis=0)
              for i in n]
        y_pairs = [ah[i][RCHUNK:]
                   + _dot(jnp.concatenate([gm_e[i][RCHUNK:], gm_o[i][RCHUNK:]], axis=1), uv[i])
                   for i in n]
        new_state = [(state[i] + _dot_tn(uv[i], bk[i])) * wend[i] for i in n]
        yv = jnp.concatenate([jnp.concatenate(y_pairs[b * n_pairs:(b + 1) * n_pairs], axis=1)
                              for b in range(nb)], axis=0)
        return yv, new_state

    def finish(c, yv, ops):
        inv = 1.0 / HEAD_DIM
        mean = head_sum(yv) * inv
        d = yv - mean
        var = head_sum(d * d) * inv
        yn = d * lax.rsqrt(var + GN_EPS) * lg_ref[...] + lb_ref[...]
        yb = ((yn + ops["bonus"]) * ops["gate"]).astype(y_ref.dtype)
        for b in range(nb):
            y_ref[b, c * RCHUNK:(c + 1) * RCHUNK, MIX_A:] = yb[b * RCHUNK:(b + 1) * RCHUNK]

    state = [state_ref[i] for i in range(nb * n_pairs)]
    vg_prev = None
    for c in range(ts // RCHUNK):
        ops, vg_prev = prepare(c, vg_prev)
        yv, state = recur(ops, state)
        finish(c, yv, ops)
    for i in range(nb * n_pairs):
        state_ref[i] = state[i]


def _mix_call(x, mod3, eg, eb, win_p, mu_p, sgg, sgb, ws, bs_b, w0, wwa, a0, gup_p, kkp, kap, rkp,
              lg, lb):
    B, S, D = x.shape
    nb = math.gcd(B, MIX_BATCH)
    ts = SEQ_TILE
    assert ts % GMLP_CHUNK == 0 and S % ts == 0
    const2 = lambda b, t: (0, 0)
    const3 = lambda b, t: (0, 0, 0)

    def full(a):
        return pl.BlockSpec(a.shape, const2 if a.ndim == 2 else const3)

    in_specs = [pl.BlockSpec((nb, ts, D), lambda b, t: (b, t, 0)),
                pl.BlockSpec((nb, 6, D), lambda b, t: (b, 0, 0))]
    consts = [eg, eb, win_p, mu_p, sgg, sgb, ws, bs_b, w0, wwa, a0, gup_p, kkp, kap, rkp, lg, lb]
    in_specs += [full(a) for a in consts]
    scratch = [pltpu.VMEM((8 * nb, RWKV_COLS_PAD), _F32),
               pltpu.VMEM((nb * HEADS // 2, PAIR, PAIR), _F32)]
    return pl.pallas_call(
        _mix_kernel,
        grid=(B // nb, S // ts),
        in_specs=in_specs,
        out_specs=pl.BlockSpec((nb, ts, D), lambda b, t: (b, t, 0)),
        out_shape=jax.ShapeDtypeStruct((B, S, D), _BF16),
        scratch_shapes=scratch,
        compiler_params=pltpu.CompilerParams(
            dimension_semantics=("arbitrary", "arbitrary"),
            vmem_limit_bytes=VMEM_LIMIT_BYTES),
        name="token_mix",
    )(x, mod3, *consts)


def _ffn_kernel(x_ref, y_ref, mod_ref, eg_ref, eb_ref, wout_ref, g1_ref, b1_ref, wfi_ref, wfo_ref,
                g2_ref, b2_ref, o_ref, act_ref):
    gt1 = mod_ref[0, 2:3, :]
    sh2 = mod_ref[0, 3:4, :]
    sc2 = mod_ref[0, 4:5, :]
    gt2 = mod_ref[0, 5:6, :]
    tm = x_ref.shape[1]
    slabs = [slice(i, i + tm // FFN_SLABS) for i in range(0, tm, tm // FFN_SLABS)]

    def pre(s):
        x0 = _layer_norm(x_ref[0, s, :], eg_ref[...], eb_ref[...], LN_EPS)
        mix = jnp.dot(y_ref[0, s, :], wout_ref[...], preferred_element_type=_F32)
        x1 = _layer_norm(ALPHA * x0 + gt1 * mix, g1_ref[...], b1_ref[...], LN_EPS)
        return x1, (x1 * (1.0 + sc2) + sh2).astype(_BF16)

    def hidden(s, h):
        for c in range(FFN_HIDDEN // FFN_CHUNK):
            gcol = pl.ds(c * FFN_CHUNK, FFN_CHUNK)
            ucol = pl.ds(FFN_HIDDEN + c * FFN_CHUNK, FFN_CHUNK)
            gg = jnp.dot(h, wfi_ref[:, gcol], preferred_element_type=_F32)
            uu = jnp.dot(h, wfi_ref[:, ucol], preferred_element_type=_F32)
            act_ref[s, gcol] = (gg * jax.nn.sigmoid(gg) * uu).astype(_BF16)

    def post(s, x1):
        f = jnp.dot(act_ref[s, :], wfo_ref[...], preferred_element_type=_F32)
        o_ref[0, s, :] = _layer_norm(ALPHA * x1 + gt2 * f, g2_ref[...], b2_ref[...], LN_EPS)

    nxt = pre(slabs[0])
    for i, s in enumerate(slabs):
        x1, h = nxt
        if i + 1 < FFN_SLABS:
            nxt = pre(slabs[i + 1])
        hidden(s, h)
        post(s, x1)


def _ffn_call(x, y, mod3, eg, eb, wout, g1, b1, wfi, wfo, g2, b2):
    B, S, D = x.shape
    tm = FFN_TILE
    const2 = lambda b, t: (0, 0)
    tile = lambda b, t: (b, t, 0)

    def resident(a):
        return pl.BlockSpec(a.shape, const2, pipeline_mode=pl.Buffered(1))

    return pl.pallas_call(
        _ffn_kernel,
        grid=(B, S // tm),
        in_specs=[pl.BlockSpec((1, tm, D), tile),
                  pl.BlockSpec((1, tm, D), tile),
                  pl.BlockSpec((1, 6, D), lambda b, t: (b, 0, 0)),
                  resident(eg), resident(eb), resident(wout), resident(g1), resident(b1),
                  resident(wfi), resident(wfo), resident(g2), resident(b2)],
        out_specs=pl.BlockSpec((1, tm, D), tile),
        out_shape=jax.ShapeDtypeStruct((B, S, D), _F32),
        scratch_shapes=[pltpu.VMEM((tm, FFN_HIDDEN), _BF16)],
        compiler_params=pltpu.CompilerParams(
            dimension_semantics=("arbitrary", "arbitrary"),
            vmem_limit_bytes=VMEM_LIMIT_BYTES),
        name="proj_ffn",
    )(x, y, mod3, eg, eb, wout, g1, b1, wfi, wfo, g2, b2)


def kernel(x, c, emb_ln_g, emb_ln_b, w_ada, b_ada, w_in, mu_shift, sg_ln_g, sg_ln_b, w_s, b_s,
           w0, w_up, a0, a_up, g_up, k_k, k_a, r_k, lnx_g, lnx_b, w_out, ln1_g, ln1_b,
           w_ffn_in, w_ffn_out, ln2_g, ln2_b):
    B, S, D = x.shape
    assert w_in.shape[0] == DEPTH == 1
    row = lambda a: a.reshape(1, -1).astype(_F32)
    layer = lambda a: a.reshape(a.shape[1:])
    pad_cols = RWKV_COLS_PAD - RWKV_COLS

    win_p = jnp.concatenate([layer(w_in).astype(_BF16), jnp.zeros((D, pad_cols), _BF16)], axis=1)
    mu_p = jnp.pad(layer(mu_shift), (0, pad_cols)).reshape(1, -1)
    wwa = jnp.zeros((LORA_W + LORA_A, 2 * MIX_B), _F32)
    wwa = wwa.at[:LORA_W, :MIX_B].set(layer(w_up)).at[LORA_W:, MIX_B:].set(layer(a_up))
    wwa = wwa.astype(_BF16)
    gup_p = jnp.pad(layer(g_up), ((0, LORA_G_PAD - LORA_G), (0, 0))).astype(_BF16)
    bs_b = jnp.broadcast_to(layer(b_s)[:, :, None], (A_GROUPS, GMLP_CHUNK, GMLP_CHUNK))

    mod3 = _ada_call(c, layer(w_ada), layer(b_ada)).reshape(B, 6, D)
    y = _mix_call(x, mod3, row(emb_ln_g), row(emb_ln_b), win_p, mu_p, row(sg_ln_g),
                  row(sg_ln_b), layer(w_s), bs_b, row(w0), wwa, row(a0), gup_p,
                  row(k_k), row(k_a), row(r_k), row(lnx_g), row(lnx_b))
    return _ffn_call(x, y, mod3, row(emb_ln_g), row(emb_ln_b), layer(w_out).astype(_BF16),
                     row(ln1_g), row(ln1_b), layer(w_ffn_in).astype(_BF16),
                     layer(w_ffn_out).astype(_BF16), row(ln2_g), row(ln2_b))
```

```python
import math

import jax
import jax.numpy as jnp
from jax import lax
from jax.experimental import pallas as pl
from jax.experimental.pallas import tpu as pltpu

D_MODEL = 1024
MIX_A = 512
MIX_B = 512
GMLP_CHUNK = 128
A_GROUPS = 4
HEAD_DIM = 64
HEADS = MIX_B // HEAD_DIM
LORA_W = 64
LORA_A = 64
LORA_G = 160
LORA_G_PAD = 256
RWKV_COLS = 3 * MIX_B + LORA_W + LORA_A + LORA_G
RWKV_COLS_PAD = 3 * MIX_B + LORA_W + LORA_A + LORA_G_PAD
IN_COLS_PAD = 2 * MIX_A + RWKV_COLS_PAD
FFN_HIDDEN = 2816
FFN_CHUNK = 256
DEPTH = 1
ALPHA = (2 * DEPTH) ** 0.25
LN_EPS = 1e-5
GN_EPS = 64e-5
DECAY_SCALE = math.exp(-0.5)

RCHUNK = 64
PAIR = 2 * HEAD_DIM
MIX_BATCH = 4
SEQ_TILE = 256
FFN_TILE = 1024
FFN_SLABS = 4
VMEM_LIMIT_BYTES = 56 * 1024 * 1024

_F32 = jnp.float32
_BF16 = jnp.bfloat16


def _layer_norm(x, g, b, eps):
    mu = jnp.mean(x, axis=-1, keepdims=True)
    d = x - mu
    var = jnp.mean(d * d, axis=-1, keepdims=True)
    return d * lax.rsqrt(var + eps) * g + b


def _gelu_tanh(x):
    c = math.sqrt(2.0 / math.pi)
    half = 0.5 * x
    return half + half * jnp.tanh(x * (c + (c * 0.044715) * (x * x)))


def _dot(a, b):
    return jnp.dot(a.astype(_BF16), b.astype(_BF16), preferred_element_type=_F32)


def _dot_nt(a, b):
    return lax.dot_general(a.astype(_BF16), b.astype(_BF16), (((1,), (1,)), ((), ())),
                           preferred_element_type=_F32)


def _dot_tn(a, b):
    return lax.dot_general(a.astype(_BF16), b.astype(_BF16), (((0,), (0,)), ((), ())),
                           preferred_element_type=_F32)


def _ada_kernel(c_ref, w_ref, b_ref, o_ref):
    c = c_ref[...]
    cs = c * jax.nn.sigmoid(c)
    o_ref[...] = _dot(cs, w_ref[...]) + b_ref[...]


def _ada_call(c, w_ada, b_ada):
    B, D = c.shape
    n = w_ada.shape[1]
    tn = 1024
    return pl.pallas_call(
        _ada_kernel,
        grid=(n // tn,),
        in_specs=[pl.BlockSpec((B, D), lambda j: (0, 0)),
                  pl.BlockSpec((D, tn), lambda j: (0, j)),
                  pl.BlockSpec((1, tn), lambda j: (0, j))],
        out_specs=pl.BlockSpec((B, tn), lambda j: (0, j)),
        out_shape=jax.ShapeDtypeStruct((B, n), _F32),
        name="ada_mod",
    )(c, w_ada, b_ada.reshape(1, n))


def _mix_kernel(x_ref, mod_ref, eg_ref, eb_ref, win_ref, mu_ref, sgg_ref, sgb_ref, ws_ref, bs_ref,
                w0_ref, wwa_ref, a0_ref, gup_ref, kk_ref, ka_ref, rk_ref, lg_ref, lb_ref,
                y_ref, carry_ref, state_ref):
    nb, ts = x_ref.shape[0], x_ref.shape[1]
    rows = nb * RCHUNK
    n_pairs = HEADS // 2

    @pl.when(pl.program_id(1) == 0)
    def _():
        carry_ref[...] = jnp.zeros_like(carry_ref)
        state_ref[...] = jnp.zeros_like(state_ref)

    hr = jnp.right_shift(lax.broadcasted_iota(jnp.int32, (2 * PAIR, 2 * PAIR), 0), 6)
    hc = jnp.right_shift(lax.broadcasted_iota(jnp.int32, (2 * PAIR, 2 * PAIR), 1), 6)
    head_ones = jnp.where(hr == hc, 1.0, 0.0).astype(_BF16)

    def head_sums(*arrays):
        halves = [a[:, i:i + 2 * PAIR] for a in arrays for i in range(0, MIX_B, 2 * PAIR)]
        out = _dot(jnp.concatenate(halves, axis=0), head_ones)
        m = arrays[0].shape[0]
        parts = [out[j * m:(j + 1) * m] for j in range(len(halves))]
        per = MIX_B // (2 * PAIR)
        return [jnp.concatenate(parts[j * per:(j + 1) * per], axis=1)
                for j in range(len(arrays))]

    def head_sum(a):
        return head_sums(a)[0]

    row = lax.broadcasted_iota(jnp.int32, (GMLP_CHUNK, GMLP_CHUNK), 0)
    col = lax.broadcasted_iota(jnp.int32, (GMLP_CHUNK, GMLP_CHUNK), 1)
    wm = [jnp.where(row >= col, ws_ref[g], 0.0).astype(_BF16) for g in range(A_GROUPS)]

    ri = lax.broadcasted_iota(jnp.int32, (rows, rows), 0)
    ci = lax.broadcasted_iota(jnp.int32, (rows, rows), 1)
    lmat = jnp.where((jnp.right_shift(ri, 6) == jnp.right_shift(ci, 6)) & (ri >= ci),
                     1.0, 0.0).astype(_BF16)

    gi = lax.broadcasted_iota(jnp.int32, (PAIR, PAIR), 0)
    gj = lax.broadcasted_iota(jnp.int32, (PAIR, PAIR), 1)
    gim = jnp.bitwise_and(gi, RCHUNK - 1)
    gjm = jnp.bitwise_and(gj, RCHUNK - 1)
    gmask = (gim > gjm) | ((gi >= RCHUNK) & (gim == gjm))
    bdmask = (gi < RCHUNK) == (gj < RCHUNK)
    left = gj < RCHUNK
    left_c = lax.broadcasted_iota(jnp.int32, (RCHUNK, PAIR), 1) < RCHUNK

    def lsel(a):
        return jnp.where(left if a.shape[0] == PAIR else left_c, a, jnp.zeros_like(a))

    def rsel(a):
        return jnp.where(left if a.shape[0] == PAIR else left_c, jnp.zeros_like(a), a)

    def prepare(c, vg_prev):
        tok = slice(c * RCHUNK, (c + 1) * RCHUNK)
        h_parts = []
        for b in range(nb):
            x0 = _layer_norm(x_ref[b, tok, :], eg_ref[...], eb_ref[...], LN_EPS)
            h_parts.append((x0 * (1.0 + mod_ref[b, 1:2, :]) + mod_ref[b, 0:1, :]).astype(_BF16))
        h = jnp.concatenate(h_parts, axis=0)
        p = jnp.dot(h, win_ref[...], preferred_element_type=_F32)

        u = _gelu_tanh(p[:, 0:MIX_A])
        vg = _layer_norm(_gelu_tanh(p[:, MIX_A:2 * MIX_A]), sgg_ref[...], sgb_ref[...],
                         LN_EPS).astype(_BF16)
        second = c % 2 == 1
        hrow = slice(RCHUNK, 2 * RCHUNK) if second else slice(0, RCHUNK)
        bslice = [slice(b * RCHUNK, (b + 1) * RCHUNK) for b in range(nb)]
        for g in range(A_GROUPS):
            cs = slice(g * GMLP_CHUNK, (g + 1) * GMLP_CHUNK)
            if second:
                vcat = jnp.concatenate(
                    [jnp.concatenate([vg_prev[rs, cs], vg[rs, cs]], axis=0) for rs in bslice],
                    axis=1)
                s_all = jnp.dot(wm[g][hrow, :], vcat, preferred_element_type=_F32)
            else:
                vcat = jnp.concatenate([vg[rs, cs] for rs in bslice], axis=1)
                s_all = jnp.dot(wm[g][hrow, 0:RCHUNK], vcat, preferred_element_type=_F32)
            for b, rs in enumerate(bslice):
                s = s_all[:, b * GMLP_CHUNK:(b + 1) * GMLP_CHUNK] + bs_ref[g, hrow, :]
                y_ref[b, tok, cs] = (u[rs, cs] * s).astype(y_ref.dtype)

        ps = p[:, 2 * MIX_A:]
        prev = pltpu.roll(ps, 1, 0)
        rid = lax.broadcasted_iota(jnp.int32, ps.shape, 0)
        for b in range(nb):
            prev = jnp.where(rid == b * RCHUNK, carry_ref[8 * b:8 * b + 1, :], prev)
        for b in range(nb):
            carry_ref[8 * b:8 * b + 1, :] = ps[(b + 1) * RCHUNK - 1:(b + 1) * RCHUNK, :]
        xs = ps + (prev - ps) * mu_ref[...]

        r = xs[:, 0:MIX_B]
        k = xs[:, MIX_B:2 * MIX_B]
        v = xs[:, 2 * MIX_B:3 * MIX_B]
        wa = xs[:, 3 * MIX_B:3 * MIX_B + 128]
        gd = xs[:, 3 * MIX_B + 128:]

        lane = lax.broadcasted_iota(jnp.int32, wa.shape, 1)
        wa_act = jnp.where(lane < LORA_W, jnp.tanh(wa), wa)
        up = _dot(wa_act, wwa_ref[...])
        sig_w = jax.nn.sigmoid(w0_ref[...] + up[:, 0:MIX_B])
        lr = jax.nn.sigmoid(a0_ref[...] + up[:, MIX_B:])
        gate = _dot(jax.nn.sigmoid(gd), gup_ref[...])

        kk = k * kk_ref[...]
        k2 = k * (1.0 + (lr - 1.0) * ka_ref[...])
        kk_ss, rk_sum = head_sums(kk * kk, r * k2 * rk_ref[...])
        kk = kk * lax.rsqrt(jnp.maximum(kk_ss, 1e-24))
        bonus = rk_sum * v

        sw_hi = sig_w.astype(_BF16)
        sw_lo = (sig_w - sw_hi.astype(_F32)).astype(_BF16)
        csum2 = jnp.dot(lmat, jnp.concatenate([sw_hi, sw_lo], axis=1),
                        preferred_element_type=_F32)
        csum = csum2[:, 0:MIX_B] + csum2[:, MIX_B:]
        cum = -DECAY_SCALE * csum
        e_pos = jnp.exp(cum)
        e_neg = jnp.exp(-cum)
        e_exc = jnp.exp(cum + DECAY_SCALE * sig_w)
        ops = dict(at=(-kk * e_exc).astype(_BF16), rt=(r * e_pos).astype(_BF16),
                   bt=(kk * lr * e_neg).astype(_BF16), kt=(k2 * e_neg).astype(_BF16),
                   v=v.astype(_BF16), e_pos=e_pos, bonus=bonus, gate=gate)
        return ops, vg

    def recur(ops, state):
        lp, r_e, r_o, vp, wend = [], [], [], [], []
        for b in range(nb):
            rs = slice(b * RCHUNK, (b + 1) * RCHUNK)
            last = (b + 1) * RCHUNK - 1
            for q in range(n_pairs):
                ls = slice(q * PAIR, (q + 1) * PAIR)
                lp.append(jnp.concatenate([ops["at"][rs, ls], ops["rt"][rs, ls]], axis=0))
                r_e.append(jnp.concatenate([ops["kt"][rs, ls], ops["bt"][rs, ls]], axis=0))
                r_o.append(jnp.concatenate([ops["bt"][rs, ls], ops["kt"][rs, ls]], axis=0))
                vp.append(ops["v"][rs, ls])
                wend.append(ops["e_pos"][last:last + 1, ls])
        n = range(len(lp))
        bk = [jnp.concatenate([lsel(r_e[i]), rsel(r_o[i])], axis=0) for i in n]
        gah = [_dot_nt(lp[i], jnp.concatenate([bk[i], state[i].astype(_BF16)], axis=0))
               for i in n]
        ah = [gah[i][:, 2 * PAIR:] for i in n]
        gm_e = [jnp.where(gmask, gah[i][:, 0:PAIR], 0.0).astype(_BF16) for i in n]
        gm_o = [jnp.where(gmask, gah[i][:, PAIR:2 * PAIR], 0.0).astype(_BF16) for i in n]
        vbd = [jnp.concatenate([lsel(vp[i]), rsel(vp[i])], axis=0) for i in n]
        akv = [_dot(jnp.where(left_c, gm_e[i][0:RCHUNK], gm_o[i][0:RCHUNK]), vbd[i]) for i in n]
        x0 = [ah[i][0:RCHUNK] + akv[i] for i in n]
        z = [jnp.concatenate([jnp.where(left_c, gm_o[i][0:RCHUNK].astype(_F32), x0[i]),
                              jnp.where(left_c, x0[i], gm_e[i][0:RCHUNK].astype(_F32))], axis=0)
             for i in n]
        for _ in range(6):
            zb = [z[i].astype(_BF16) for i in n]
            prod = [_dot(jnp.where(bdmask, zb[i], jnp.zeros_like(zb[i])), zb[i]) for i in n]
            z = [prod[i] + jnp.where(bdmask, 0.0, z[i]) for i in n]
        uv = [jnp.concatenate([lsel(vp[i]), lsel(z[i][RCHUNK:].astype(_BF16)),
                               rsel(z[i][0:RCHUNK].astype(_BF16)), rsel(vp[i])], axis=0)
              for i in n]
        y_pairs = [ah[i][RCHUNK:]
                   + _dot(jnp.concatenate([gm_e[i][RCHUNK:], gm_o[i][RCHUNK:]], axis=1), uv[i])
                   for i in n]
        new_state = [(state[i] + _dot_tn(uv[i], bk[i])) * wend[i] for i in n]
        yv = jnp.concatenate([jnp.concatenate(y_pairs[b * n_pairs:(b + 1) * n_pairs], axis=1)
                              for b in range(nb)], axis=0)
        return yv, new_state

    def finish(c, yv, ops):
        inv = 1.0 / HEAD_DIM
        mean = head_sum(yv) * inv
        d = yv - mean
        var = head_sum(d * d) * inv
        yn = d * lax.rsqrt(var + GN_EPS) * lg_ref[...] + lb_ref[...]
        yb = ((yn + ops["bonus"]) * ops["gate"]).astype(y_ref.dtype)
        for b in range(nb):
            y_ref[b, c * RCHUNK:(c + 1) * RCHUNK, MIX_A:] = yb[b * RCHUNK:(b + 1) * RCHUNK]

    state = [state_ref[i] for i in range(nb * n_pairs)]
    vg_prev = None
    for c in range(ts // RCHUNK):
        ops, vg_prev = prepare(c, vg_prev)
        yv, state = recur(ops, state)
        finish(c, yv, ops)
    for i in range(nb * n_pairs):
        state_ref[i] = state[i]


def _mix_call(x, mod3, eg, eb, win_p, mu_p, sgg, sgb, ws, bs_b, w0, wwa, a0, gup_p, kkp, kap, rkp,
              lg, lb):
    B, S, D = x.shape
    nb = math.gcd(B, MIX_BATCH)
    ts = SEQ_TILE
    assert ts % GMLP_CHUNK == 0 and S % ts == 0
    const2 = lambda b, t: (0, 0)
    const3 = lambda b, t: (0, 0, 0)

    def full(a):
        return pl.BlockSpec(a.shape, const2 if a.ndim == 2 else const3)

    in_specs = [pl.BlockSpec((nb, ts, D), lambda b, t: (b, t, 0)),
                pl.BlockSpec((nb, 6, D), lambda b, t: (b, 0, 0))]
    consts = [eg, eb, win_p, mu_p, sgg, sgb, ws, bs_b, w0, wwa, a0, gup_p, kkp, kap, rkp, lg, lb]
    in_specs += [full(a) for a in consts]
    scratch = [pltpu.VMEM((8 * nb, RWKV_COLS_PAD), _F32),
               pltpu.VMEM((nb * HEADS // 2, PAIR, PAIR), _F32)]
    return pl.pallas_call(
        _mix_kernel,
        grid=(B // nb, S // ts),
        in_specs=in_specs,
        out_specs=pl.BlockSpec((nb, ts, D), lambda b, t: (b, t, 0)),
        out_shape=jax.ShapeDtypeStruct((B, S, D), _BF16),
        scratch_shapes=scratch,
        compiler_params=pltpu.CompilerParams(
            dimension_semantics=("arbitrary", "arbitrary"),
            vmem_limit_bytes=VMEM_LIMIT_BYTES),
        name="token_mix",
    )(x, mod3, *consts)


def _ffn_kernel(x_ref, y_ref, mod_ref, eg_ref, eb_ref, wout_ref, g1_ref, b1_ref, wfi_ref, wfo_ref,
                g2_ref, b2_ref, o_ref, act_ref):
    gt1 = mod_ref[0, 2:3, :]
    sh2 = mod_ref[0, 3:4, :]
    sc2 = mod_ref[0, 4:5, :]
    gt2 = mod_ref[0, 5:6, :]
    tm = x_ref.shape[1]
    slabs = [slice(i, i + tm // FFN_SLABS) for i in range(0, tm, tm // FFN_SLABS)]

    def pre(s):
        x0 = _layer_norm(x_ref[0, s, :], eg_ref[...], eb_ref[...], LN_EPS)
        mix = jnp.dot(y_ref[0, s, :], wout_ref[...], preferred_element_type=_F32)
        x1 = _layer_norm(ALPHA * x0 + gt1 * mix, g1_ref[...], b1_ref[...], LN_EPS)
        return x1, (x1 * (1.0 + sc2) + sh2).astype(_BF16)

    def hidden(s, h):
        for c in range(FFN_HIDDEN // FFN_CHUNK):
            gcol = pl.ds(c * FFN_CHUNK, FFN_CHUNK)
            ucol = pl.ds(FFN_HIDDEN + c * FFN_CHUNK, FFN_CHUNK)
            gg = jnp.dot(h, wfi_ref[:, gcol], preferred_element_type=_F32)
            uu = jnp.dot(h, wfi_ref[:, ucol], preferred_element_type=_F32)
            act_ref[s, gcol] = (gg * jax.nn.sigmoid(gg) * uu).astype(_BF16)

    def post(s, x1):
        f = jnp.dot(act_ref[s, :], wfo_ref[...], preferred_element_type=_F32)
        o_ref[0, s, :] = _layer_norm(ALPHA * x1 + gt2 * f, g2_ref[...], b2_ref[...], LN_EPS)

    nxt = pre(slabs[0])
    pending = None
    for i, s in enumerate(slabs):
        x1, h = nxt
        if i + 1 < FFN_SLABS:
            nxt = pre(slabs[i + 1])
        hidden(s, h)
        if pending is not None:
            post(*pending)
        pending = (s, x1)
    post(*pending)


def _ffn_call(x, y, mod3, eg, eb, wout, g1, b1, wfi, wfo, g2, b2):
    B, S, D = x.shape
    tm = FFN_TILE
    const2 = lambda b, t: (0, 0)
    tile = lambda b, t: (b, t, 0)

    def resident(a):
        return pl.BlockSpec(a.shape, const2, pipeline_mode=pl.Buffered(1))

    return pl.pallas_call(
        _ffn_kernel,
        grid=(B, S // tm),
        in_specs=[pl.BlockSpec((1, tm, D), tile),
                  pl.BlockSpec((1, tm, D), tile),
                  pl.BlockSpec((1, 6, D), lambda b, t: (b, 0, 0)),
                  resident(eg), resident(eb), resident(wout), resident(g1), resident(b1),
                  resident(wfi), resident(wfo), resident(g2), resident(b2)],
        out_specs=pl.BlockSpec((1, tm, D), tile),
        out_shape=jax.ShapeDtypeStruct((B, S, D), _F32),
        scratch_shapes=[pltpu.VMEM((tm, FFN_HIDDEN), _BF16)],
        compiler_params=pltpu.CompilerParams(
            dimension_semantics=("arbitrary", "arbitrary"),
            vmem_limit_bytes=VMEM_LIMIT_BYTES),
        name="proj_ffn",
    )(x, y, mod3, eg, eb, wout, g1, b1, wfi, wfo, g2, b2)


def kernel(x, c, emb_ln_g, emb_ln_b, w_ada, b_ada, w_in, mu_shift, sg_ln_g, sg_ln_b, w_s, b_s,
           w0, w_up, a0, a_up, g_up, k_k, k_a, r_k, lnx_g, lnx_b, w_out, ln1_g, ln1_b,
           w_ffn_in, w_ffn_out, ln2_g, ln2_b):
    B, S, D = x.shape
    assert w_in.shape[0] == DEPTH == 1
    row = lambda a: a.reshape(1, -1).astype(_F32)
    layer = lambda a: a.reshape(a.shape[1:])
    pad_cols = RWKV_COLS_PAD - RWKV_COLS

    win_p = jnp.concatenate([layer(w_in).astype(_BF16), jnp.zeros((D, pad_cols), _BF16)], axis=1)
    mu_p = jnp.pad(layer(mu_shift), (0, pad_cols)).reshape(1, -1)
    wwa = jnp.zeros((LORA_W + LORA_A, 2 * MIX_B), _F32)
    wwa = wwa.at[:LORA_W, :MIX_B].set(layer(w_up)).at[LORA_W:, MIX_B:].set(layer(a_up))
    wwa = wwa.astype(_BF16)
    gup_p = jnp.pad(layer(g_up), ((0, LORA_G_PAD - LORA_G), (0, 0))).astype(_BF16)
    bs_b = jnp.broadcast_to(layer(b_s)[:, :, None], (A_GROUPS, GMLP_CHUNK, GMLP_CHUNK))

    mod3 = _ada_call(c, layer(w_ada), layer(b_ada)).reshape(B, 6, D)
    y = _mix_call(x, mod3, row(emb_ln_g), row(emb_ln_b), win_p, mu_p, row(sg_ln_g),
                  row(sg_ln_b), layer(w_s), bs_b, row(w0), wwa, row(a0), gup_p,
                  row(k_k), row(k_a), row(r_k), row(lnx_g), row(lnx_b))
    return _ffn_call(x, y, mod3, row(emb_ln_g), row(emb_ln_b), layer(w_out).astype(_BF16),
                     row(ln1_g), row(ln1_b), layer(w_ffn_in).astype(_BF16),
                     layer(w_ffn_out).astype(_BF16), row(ln2_g), row(ln2_b))
```

```python
import math

import jax
import jax.numpy as jnp
from jax import lax
from jax.experimental import pallas as pl
from jax.experimental.pallas import tpu as pltpu

D_MODEL = 1024
MIX_A = 512
MIX_B = 512
GMLP_CHUNK = 128
A_GROUPS = 4
HEAD_DIM = 64
HEADS = MIX_B // HEAD_DIM
LORA_W = 64
LORA_A = 64
LORA_G = 160
LORA_G_PAD = 256
RWKV_COLS = 3 * MIX_B + LORA_W + LORA_A + LORA_G
RWKV_COLS_PAD = 3 * MIX_B + LORA_W + LORA_A + LORA_G_PAD
IN_COLS_PAD = 2 * MIX_A + RWKV_COLS_PAD
FFN_HIDDEN = 2816
FFN_CHUNK = 256
DEPTH = 1
ALPHA = (2 * DEPTH) ** 0.25
LN_EPS = 1e-5
GN_EPS = 64e-5
DECAY_SCALE = math.exp(-0.5)

RCHUNK = 64
PAIR = 2 * HEAD_DIM
MIX_BATCH = 4
SEQ_TILE = 256
FFN_TILE = 1024
FFN_SLABS = 4
VMEM_LIMIT_BYTES = 56 * 1024 * 1024

_F32 = jnp.float32
_BF16 = jnp.bfloat16


def _layer_norm(x, g, b, eps):
    mu = jnp.mean(x, axis=-1, keepdims=True)
    d = x - mu
    var = jnp.mean(d * d, axis=-1, keepdims=True)
    return d * lax.rsqrt(var + eps) * g + b


def _gelu_tanh(x):
    c = math.sqrt(2.0 / math.pi)
    half = 0.5 * x
    return half + half * jnp.tanh(x * (c + (c * 0.044715) * (x * x)))


def _dot(a, b):
    return jnp.dot(a.astype(_BF16), b.astype(_BF16), preferred_element_type=_F32)


def _dot_nt(a, b):
    return lax.dot_general(a.astype(_BF16), b.astype(_BF16), (((1,), (1,)), ((), ())),
                           preferred_element_type=_F32)


def _dot_tn(a, b):
    return lax.dot_general(a.astype(_BF16), b.astype(_BF16), (((0,), (0,)), ((), ())),
                           preferred_element_type=_F32)


def _ada_kernel(c_ref, w_ref, b_ref, o_ref):
    c = c_ref[...]
    cs = c * jax.nn.sigmoid(c)
    o_ref[...] = _dot(cs, w_ref[...]) + b_ref[...]


def _ada_call(c, w_ada, b_ada):
    B, D = c.shape
    n = w_ada.shape[1]
    tn = 1024
    return pl.pallas_call(
        _ada_kernel,
        grid=(n // tn,),
        in_specs=[pl.BlockSpec((B, D), lambda j: (0, 0)),
                  pl.BlockSpec((D, tn), lambda j: (0, j)),
                  pl.BlockSpec((1, tn), lambda j: (0, j))],
        out_specs=pl.BlockSpec((B, tn), lambda j: (0, j)),
        out_shape=jax.ShapeDtypeStruct((B, n), _F32),
        name="ada_mod",
    )(c, w_ada, b_ada.reshape(1, n))


def _mix_kernel(x_ref, mod_ref, eg_ref, eb_ref, win_ref, mu_ref, sgg_ref, sgb_ref, ws_ref, bs_ref,
                w0_ref, wwa_ref, a0_ref, gup_ref, kk_ref, ka_ref, rk_ref, lg_ref, lb_ref,
                y_ref, carry_ref, state_ref):
    nb, ts = x_ref.shape[0], x_ref.shape[1]
    rows = nb * RCHUNK
    n_pairs = HEADS // 2

    @pl.when(pl.program_id(1) == 0)
    def _():
        carry_ref[...] = jnp.zeros_like(carry_ref)
        state_ref[...] = jnp.zeros_like(state_ref)

    hr = jnp.right_shift(lax.broadcasted_iota(jnp.int32, (2 * PAIR, 2 * PAIR), 0), 6)
    hc = jnp.right_shift(lax.broadcasted_iota(jnp.int32, (2 * PAIR, 2 * PAIR), 1), 6)
    head_ones = jnp.where(hr == hc, 1.0, 0.0).astype(_BF16)

    def head_sums(*arrays):
        halves = [a[:, i:i + 2 * PAIR] for a in arrays for i in range(0, MIX_B, 2 * PAIR)]
        out = _dot(jnp.concatenate(halves, axis=0), head_ones)
        m = arrays[0].shape[0]
        parts = [out[j * m:(j + 1) * m] for j in range(len(halves))]
        per = MIX_B // (2 * PAIR)
        return [jnp.concatenate(parts[j * per:(j + 1) * per], axis=1)
                for j in range(len(arrays))]

    def head_sum(a):
        return head_sums(a)[0]

    row = lax.broadcasted_iota(jnp.int32, (GMLP_CHUNK, GMLP_CHUNK), 0)
    col = lax.broadcasted_iota(jnp.int32, (GMLP_CHUNK, GMLP_CHUNK), 1)
    wm = [jnp.where(row >= col, ws_ref[g], 0.0).astype(_BF16) for g in range(A_GROUPS)]

    ri = lax.broadcasted_iota(jnp.int32, (rows, rows), 0)
    ci = lax.broadcasted_iota(jnp.int32, (rows, rows), 1)
    lmat = jnp.where((jnp.right_shift(ri, 6) == jnp.right_shift(ci, 6)) & (ri >= ci),
                     1.0, 0.0).astype(_BF16)

    gi = lax.broadcasted_iota(jnp.int32, (PAIR, PAIR), 0)
    gj = lax.broadcasted_iota(jnp.int32, (PAIR, PAIR), 1)
    gim = jnp.bitwise_and(gi, RCHUNK - 1)
    gjm = jnp.bitwise_and(gj, RCHUNK - 1)
    gmask = (gim > gjm) | ((gi >= RCHUNK) & (gim == gjm))
    bdmask = (gi < RCHUNK) == (gj < RCHUNK)
    left = gj < RCHUNK
    left_c = lax.broadcasted_iota(jnp.int32, (RCHUNK, PAIR), 1) < RCHUNK

    def lsel(a):
        return jnp.where(left if a.shape[0] == PAIR else left_c, a, jnp.zeros_like(a))

    def rsel(a):
        return jnp.where(left if a.shape[0] == PAIR else left_c, jnp.zeros_like(a), a)

    def prepare(c, vg_prev):
        tok = slice(c * RCHUNK, (c + 1) * RCHUNK)
        h_parts = []
        for b in range(nb):
            x0 = _layer_norm(x_ref[b, tok, :], eg_ref[...], eb_ref[...], LN_EPS)
            h_parts.append((x0 * (1.0 + mod_ref[b, 1:2, :]) + mod_ref[b, 0:1, :]).astype(_BF16))
        h = jnp.concatenate(h_parts, axis=0)
        p = jnp.dot(h, win_ref[...], preferred_element_type=_F32)

        u = _gelu_tanh(p[:, 0:MIX_A])
        vg = _layer_norm(_gelu_tanh(p[:, MIX_A:2 * MIX_A]), sgg_ref[...], sgb_ref[...],
                         LN_EPS).astype(_BF16)
        second = c % 2 == 1
        hrow = slice(RCHUNK, 2 * RCHUNK) if second else slice(0, RCHUNK)
        bslice = [slice(b * RCHUNK, (b + 1) * RCHUNK) for b in range(nb)]
        for g in range(A_GROUPS):
            cs = slice(g * GMLP_CHUNK, (g + 1) * GMLP_CHUNK)
            if second:
                vcat = jnp.concatenate(
                    [jnp.concatenate([vg_prev[rs, cs], vg[rs, cs]], axis=0) for rs in bslice],
                    axis=1)
                s_all = jnp.dot(wm[g][hrow, :], vcat, preferred_element_type=_F32)
            else:
                vcat = jnp.concatenate([vg[rs, cs] for rs in bslice], axis=1)
                s_all = jnp.dot(wm[g][hrow, 0:RCHUNK], vcat, preferred_element_type=_F32)
            for b, rs in enumerate(bslice):
                s = s_all[:, b * GMLP_CHUNK:(b + 1) * GMLP_CHUNK] + bs_ref[g, hrow, :]
                y_ref[b, tok, cs] = (u[rs, cs] * s).astype(y_ref.dtype)

        ps = p[:, 2 * MIX_A:]
        prev = pltpu.roll(ps, 1, 0)
        rid = lax.broadcasted_iota(jnp.int32, ps.shape, 0)
        for b in range(nb):
            prev = jnp.where(rid == b * RCHUNK, carry_ref[8 * b:8 * b + 1, :], prev)
        for b in range(nb):
            carry_ref[8 * b:8 * b + 1, :] = ps[(b + 1) * RCHUNK - 1:(b + 1) * RCHUNK, :]
        xs = ps + (prev - ps) * mu_ref[...]

        r = xs[:, 0:MIX_B]
        k = xs[:, MIX_B:2 * MIX_B]
        v = xs[:, 2 * MIX_B:3 * MIX_B]
        wa = xs[:, 3 * MIX_B:3 * MIX_B + 128]
        gd = xs[:, 3 * MIX_B + 128:]

        lane = lax.broadcasted_iota(jnp.int32, wa.shape, 1)
        wa_act = jnp.where(lane < LORA_W, jnp.tanh(wa), wa)
        up = _dot(wa_act, wwa_ref[...])
        sig_w = jax.nn.sigmoid(w0_ref[...] + up[:, 0:MIX_B])
        lr = jax.nn.sigmoid(a0_ref[...] + up[:, MIX_B:])
        gate = _dot(jax.nn.sigmoid(gd), gup_ref[...])

        kk = k * kk_ref[...]
        k2 = k * (1.0 + (lr - 1.0) * ka_ref[...])
        kk_ss, rk_sum = head_sums(kk * kk, r * k2 * rk_ref[...])
        kk = kk * lax.rsqrt(jnp.maximum(kk_ss, 1e-24))
        bonus = rk_sum * v

        sw_hi = sig_w.astype(_BF16)
        sw_lo = (sig_w - sw_hi.astype(_F32)).astype(_BF16)
        csum2 = jnp.dot(lmat, jnp.concatenate([sw_hi, sw_lo], axis=1),
                        preferred_element_type=_F32)
        csum = csum2[:, 0:MIX_B] + csum2[:, MIX_B:]
        cum = -DECAY_SCALE * csum
        e_pos = jnp.exp(cum)
        e_neg = jnp.exp(-cum)
        e_exc = jnp.exp(cum + DECAY_SCALE * sig_w)
        ops = dict(at=(-kk * e_exc).astype(_BF16), rt=(r * e_pos).astype(_BF16),
                   bt=(kk * lr * e_neg).astype(_BF16), kt=(k2 * e_neg).astype(_BF16),
                   v=v.astype(_BF16), e_pos=e_pos, bonus=bonus, gate=gate)
        return ops, vg

    def recur(ops, state):
        lp, r_e, r_o, vp, wend = [], [], [], [], []
        for b in range(nb):
            rs = slice(b * RCHUNK, (b + 1) * RCHUNK)
            last = (b + 1) * RCHUNK - 1
            for q in range(n_pairs):
                ls = slice(q * PAIR, (q + 1) * PAIR)
                lp.append(jnp.concatenate([ops["at"][rs, ls], ops["rt"][rs, ls]], axis=0))
                r_e.append(jnp.concatenate([ops["kt"][rs, ls], ops["bt"][rs, ls]], axis=0))
                r_o.append(jnp.concatenate([ops["bt"][rs, ls], ops["kt"][rs, ls]], axis=0))
                vp.append(ops["v"][rs, ls])
                wend.append(ops["e_pos"][last:last + 1, ls])
        n = range(len(lp))
        bk = [jnp.concatenate([lsel(r_e[i]), rsel(r_o[i])], axis=0) for i in n]
        gah = [_dot_nt(lp[i], jnp.concatenate([bk[i], state[i].astype(_BF16)], axis=0))
               for i in n]
        ah = [gah[i][:, 2 * PAIR:] for i in n]
        gm_e = [jnp.where(gmask, gah[i][:, 0:PAIR], 0.0).astype(_BF16) for i in n]
        gm_o = [jnp.where(gmask, gah[i][:, PAIR:2 * PAIR], 0.0).astype(_BF16) for i in n]
        vbd = [jnp.concatenate([lsel(vp[i]), rsel(vp[i])], axis=0) for i in n]
        akv = [_dot(jnp.where(left_c, gm_e[i][0:RCHUNK], gm_o[i][0:RCHUNK]), vbd[i]) for i in n]
        x0 = [ah[i][0:RCHUNK] + akv[i] for i in n]
        z = [jnp.concatenate([jnp.where(left_c, gm_o[i][0:RCHUNK].astype(_F32), x0[i]),
                              jnp.where(left_c, x0[i], gm_e[i][0:RCHUNK].astype(_F32))], axis=0)
             for i in n]
        for _ in range(6):
            zb = [z[i].astype(_BF16) for i in n]
            prod = [_dot(jnp.where(bdmask, zb[i], jnp.zeros_like(zb[i])), zb[i]) for i in n]
            z = [prod[i] + jnp.where(bdmask, 0.0, z[i]) for i in n]
        uv = [jnp.concatenate([lsel(vp[i]), lsel(z[i][RCHUNK:].astype(_BF16)),
                               rsel(z[i][0:RCHUNK].astype(_BF16)), rsel(vp[i])], axis=0)
              for i in n]
        y_pairs = [ah[i][RCHUNK:]
                   + _dot(jnp.concatenate([gm_e[i][RCHUNK:], gm_o[i][RCHUNK:]], axis=1), uv[i])
                   for i in n]
        new_state = [(state[i] + _dot_tn(uv[i], bk[i])) * wend[i] for i in n]
        yv = jnp.concatenate([jnp.concatenate(y_pairs[b * n_pairs:(b + 1) * n_pairs], axis=1)
                              for b in range(nb)], axis=0)
        return yv, new_state

    def finish(c, yv, ops):
        inv = 1.0 / HEAD_DIM
        mean = head_sum(yv) * inv
        d = yv - mean
        var = head_sum(d * d) * inv
        yn = d * lax.rsqrt(var + GN_EPS) * lg_ref[...] + lb_ref[...]
        yb = ((yn + ops["bonus"]) * ops["gate"]).astype(y_ref.dtype)
        for b in range(nb):
            y_ref[b, c * RCHUNK:(c + 1) * RCHUNK, MIX_A:] = yb[b * RCHUNK:(b + 1) * RCHUNK]

    state = [state_ref[i] for i in range(nb * n_pairs)]
    vg_prev = None
    for c in range(ts // RCHUNK):
        ops, vg_prev = prepare(c, vg_prev)
        yv, state = recur(ops, state)
        finish(c, yv, ops)
    for i in range(nb * n_pairs):
        state_ref[i] = state[i]


def _mix_call(x, mod3, eg, eb, win_p, mu_p, sgg, sgb, ws, bs_b, w0, wwa, a0, gup_p, kkp, kap, rkp,
              lg, lb):
    B, S, D = x.shape
    nb = math.gcd(B, MIX_BATCH)
    ts = SEQ_TILE
    assert ts % GMLP_CHUNK == 0 and S % ts == 0
    const2 = lambda b, t: (0, 0)
    const3 = lambda b, t: (0, 0, 0)

    def full(a):
        return pl.BlockSpec(a.shape, const2 if a.ndim == 2 else const3)

    in_specs = [pl.BlockSpec((nb, ts, D), lambda b, t: (b, t, 0)),
                pl.BlockSpec((nb, 6, D), lambda b, t: (b, 0, 0))]
    consts = [eg, eb, win_p, mu_p, sgg, sgb, ws, bs_b, w0, wwa, a0, gup_p, kkp, kap, rkp, lg, lb]
    in_specs += [full(a) for a in consts]
    scratch = [pltpu.VMEM((8 * nb, RWKV_COLS_PAD), _F32),
               pltpu.VMEM((nb * HEADS // 2, PAIR, PAIR), _F32)]
    return pl.pallas_call(
        _mix_kernel,
        grid=(B // nb, S // ts),
        in_specs=in_specs,
        out_specs=pl.BlockSpec((nb, ts, D), lambda b, t: (b, t, 0)),
        out_shape=jax.ShapeDtypeStruct((B, S, D), _BF16),
        scratch_shapes=scratch,
        compiler_params=pltpu.CompilerParams(
            dimension_semantics=("arbitrary", "arbitrary"),
            vmem_limit_bytes=VMEM_LIMIT_BYTES),
        name="token_mix",
    )(x, mod3, *consts)


def _ffn_kernel(x_ref, y_ref, mod_ref, eg_ref, eb_ref, wout_ref, g1_ref, b1_ref, wfi_ref, wfo_ref,
                g2_ref, b2_ref, o_ref, act_ref):
    gt1 = mod_ref[0, 2:3, :]
    sh2 = mod_ref[0, 3:4, :]
    sc2 = mod_ref[0, 4:5, :]
    gt2 = mod_ref[0, 5:6, :]
    tm = x_ref.shape[1]
    slabs = [slice(i, i + tm // FFN_SLABS) for i in range(0, tm, tm // FFN_SLABS)]

    def pre(s):
        x0 = _layer_norm(x_ref[0, s, :], eg_ref[...], eb_ref[...], LN_EPS)
        mix = jnp.dot(y_ref[0, s, :], wout_ref[...], preferred_element_type=_F32)
        x1 = _layer_norm(ALPHA * x0 + gt1 * mix, g1_ref[...], b1_ref[...], LN_EPS)
        return x1, (x1 * (1.0 + sc2) + sh2).astype(_BF16)

    def hidden(s, h):
        for c in range(FFN_HIDDEN // FFN_CHUNK):
            gcol = pl.ds(c * FFN_CHUNK, FFN_CHUNK)
            ucol = pl.ds(FFN_HIDDEN + c * FFN_CHUNK, FFN_CHUNK)
            gg = jnp.dot(h, wfi_ref[:, gcol], preferred_element_type=_F32)
            uu = jnp.dot(h, wfi_ref[:, ucol], preferred_element_type=_F32)
            act_ref[s, gcol] = (gg * jax.nn.sigmoid(gg) * uu).astype(_BF16)

    def post(s, x1):
        f = jnp.dot(act_ref[s, :], wfo_ref[...], preferred_element_type=_F32)
        o_ref[0, s, :] = _layer_norm(ALPHA * x1 + gt2 * f, g2_ref[...], b2_ref[...], LN_EPS)

    nxt = pre(slabs[0])
    pending = None
    for i, s in enumerate(slabs):
        x1, h = nxt
        hidden(s, h)
        if i + 1 < FFN_SLABS:
            nxt = pre(slabs[i + 1])
        if pending is not None:
            post(*pending)
        pending = (s, x1)
    post(*pending)


def _ffn_call(x, y, mod3, eg, eb, wout, g1, b1, wfi, wfo, g2, b2):
    B, S, D = x.shape
    tm = FFN_TILE
    const2 = lambda b, t: (0, 0)
    tile = lambda b, t: (b, t, 0)

    def resident(a):
        return pl.BlockSpec(a.shape, const2, pipeline_mode=pl.Buffered(1))

    return pl.pallas_call(
        _ffn_kernel,
        grid=(B, S // tm),
        in_specs=[pl.BlockSpec((1, tm, D), tile),
                  pl.BlockSpec((1, tm, D), tile),
                  pl.BlockSpec((1, 6, D), lambda b, t: (b, 0, 0)),
                  resident(eg), resident(eb), resident(wout), resident(g1), resident(b1),
                  resident(wfi), resident(wfo), resident(g2), resident(b2)],
        out_specs=pl.BlockSpec((1, tm, D), tile),
        out_shape=jax.ShapeDtypeStruct((B, S, D), _F32),
        scratch_shapes=[pltpu.VMEM((tm, FFN_HIDDEN), _BF16)],
        compiler_params=pltpu.CompilerParams(
            dimension_semantics=("arbitrary", "arbitrary"),
            vmem_limit_bytes=VMEM_LIMIT_BYTES),
        name="proj_ffn",
    )(x, y, mod3, eg, eb, wout, g1, b1, wfi, wfo, g2, b2)


def kernel(x, c, emb_ln_g, emb_ln_b, w_ada, b_ada, w_in, mu_shift, sg_ln_g, sg_ln_b, w_s, b_s,
           w0, w_up, a0, a_up, g_up, k_k, k_a, r_k, lnx_g, lnx_b, w_out, ln1_g, ln1_b,
           w_ffn_in, w_ffn_out, ln2_g, ln2_b):
    B, S, D = x.shape
    assert w_in.shape[0] == DEPTH == 1
    row = lambda a: a.reshape(1, -1).astype(_F32)
    layer = lambda a: a.reshape(a.shape[1:])
    pad_cols = RWKV_COLS_PAD - RWKV_COLS

    win_p = jnp.concatenate([layer(w_in).astype(_BF16), jnp.zeros((D, pad_cols), _BF16)], axis=1)
    mu_p = jnp.pad(layer(mu_shift), (0, pad_cols)).reshape(1, -1)
    wwa = jnp.zeros((LORA_W + LORA_A, 2 * MIX_B), _F32)
    wwa = wwa.at[:LORA_W, :MIX_B].set(layer(w_up)).at[LORA_W:, MIX_B:].set(layer(a_up))
    wwa = wwa.astype(_BF16)
    gup_p = jnp.pad(layer(g_up), ((0, LORA_G_PAD - LORA_G), (0, 0))).astype(_BF16)
    bs_b = jnp.broadcast_to(layer(b_s)[:, :, None], (A_GROUPS, GMLP_CHUNK, GMLP_CHUNK))

    mod3 = _ada_call(c, layer(w_ada), layer(b_ada)).reshape(B, 6, D)
    y = _mix_call(x, mod3, row(emb_ln_g), row(emb_ln_b), win_p, mu_p, row(sg_ln_g),
                  row(sg_ln_b), layer(w_s), bs_b, row(w0), wwa, row(a0), gup_p,
                  row(k_k), row(k_a), row(r_k), row(lnx_g), row(lnx_b))
    return _ffn_call(x, y, mod3, row(emb_ln_g), row(emb_ln_b), layer(w_out).astype(_BF16),
                     row(ln1_g), row(ln1_b), layer(w_ffn_in).astype(_BF16),
                     layer(w_ffn_out).astype(_BF16), row(ln2_g), row(ln2_b))
```
